```python
import jax
import jax.numpy as jnp
from jax import lax
import numpy as np

D_MODEL = 4096
BATCH = 4
SEQ = 2048
DEPTH = 4
DEC_BATCH = 8
DEC_SEQ = 1
PAST_LEN = 8192
PAGE_SIZE = 128

N_MIXERS = 2
N_GLA_LAYERS = (DEPTH + 1) // 2
N_DSA_LAYERS = DEPTH // 2
HEAD_DIM = 128
DSA_GROUPS = ((128, 1), (512, 4), (2048, 16))
N_GROUPS = len(DSA_GROUPS)
DSA_HEADS = D_MODEL // 256
DSA_WIDTH = DSA_HEADS * HEAD_DIM
ROT_DIM = HEAD_DIM // 4
ROPE_THETA = 500000.0
GLA_HEADS = D_MODEL // 512
GLA_KD = D_MODEL // 2
GLA_VD = D_MODEL
GLA_DK = GLA_KD // GLA_HEADS
GLA_DV = GLA_VD // GLA_HEADS
GLA_GATE_RANK = 16
GLA_GATE_TAU = 16.0
GLA_CHUNK = 64
GLA_IN = 2 * GLA_KD + 2 * GLA_VD + GLA_GATE_RANK
D_FF = 256 * ((8 * D_MODEL + 3 * 256 - 1) // (3 * 256))
CONV_W = 3
EPS = 1e-6
F32 = jnp.float32

kernel_name = 'hybrid_gla_dilated_swa_convffn_step'


def rms_norm(x, g):
    xf = x.astype(F32)
    y = xf * lax.rsqrt(jnp.mean(xf * xf, axis=-1, keepdims=True) + EPS)
    return (y * g.astype(F32)).astype(x.dtype)


def rope_partial(x, pos):
    half = ROT_DIM // 2
    inv = jnp.power(ROPE_THETA, -jnp.arange(half, dtype=F32) * (2.0 / ROT_DIM))
    ang = pos.astype(F32)[:, None] * inv[None, :]
    cos = jnp.cos(ang)[None, :, None, None, :]
    sin = jnp.sin(ang)[None, :, None, None, :]
    xf = x.astype(F32)
    x1 = xf[..., :half]
    x2 = xf[..., half:ROT_DIM]
    out = jnp.concatenate([x1 * cos - x2 * sin, x2 * cos + x1 * sin, xf[..., ROT_DIM:]], axis=-1)
    return out.astype(x.dtype)


def dsa_project(x, w_qkv, q_gain, k_gain, pos):
    B, L, _ = x.shape
    h = (x @ w_qkv).reshape(B, L, N_GROUPS, 3, DSA_HEADS, HEAD_DIM)
    q = rope_partial(rms_norm(h[:, :, :, 0], q_gain[:, None, :]), pos)
    k = rope_partial(rms_norm(h[:, :, :, 1], k_gain[:, None, :]), pos)
    v = h[:, :, :, 2]
    return q, k, v


def dilated_band_attention(q, k, v, window, dil):
    B, S, H, D = q.shape
    blk = window // dil
    span = blk * dil
    Sp = -(-S // span) * span
    nb = Sp // span

    def split(t):
        t = jnp.pad(t.astype(F32), ((0, 0), (0, Sp - S), (0, 0), (0, 0)))
        return t.reshape(B, nb, blk, dil, H, D)

    def with_prev(t):
        prev = jnp.pad(t[:, :-1], ((0, 0), (1, 0), (0, 0), (0, 0), (0, 0), (0, 0)))
        return jnp.concatenate([prev, t], axis=2)

    qr = split(q)
    kk = with_prev(split(k))
    vv = with_prev(split(v))
    s = jnp.einsum('bnqrhd,bnkrhd->bnrhqk', qr, kk) * (D ** -0.5)
    qi = jnp.arange(blk)[:, None]
    ki = jnp.arange(2 * blk)[None, :]
    rel = blk + qi - ki
    band = (rel >= 0) & (rel <= blk)
    mask = band[None] & ((jnp.arange(nb)[:, None, None] > 0) | (ki[None] >= blk))
    s = jnp.where(mask[None, :, None, None], s, -jnp.inf)
    mx = jnp.max(s, axis=-1, keepdims=True)
    p = jnp.exp(s - mx)
    den = jnp.sum(p, axis=-1)
    o = jnp.einsum('bnrhqk,bnkrhd->bnqrhd', p, vv) / den.transpose(0, 1, 4, 2, 3)[..., None]
    lse = (mx[..., 0] + jnp.log(den)).transpose(0, 1, 4, 2, 3)
    return o.reshape(B, Sp, H, D)[:, :S], lse.reshape(B, Sp, H)[:, :S]


def dilated_gather_attention(q, kk, vv, window, dil, n_buf):
    B, T, H, D = q.shape
    steps = jnp.arange(window // dil + 1)
    idx = n_buf + jnp.arange(T)[:, None] - dil * steps[None, :]
    valid = idx >= 0
    idx = jnp.maximum(idx, 0)
    kg = kk[:, idx].astype(F32)
    vg = vv[:, idx].astype(F32)
    s = jnp.einsum('bthd,btmhd->bhtm', q.astype(F32), kg) * (D ** -0.5)
    s = jnp.where(valid[None, None], s, -jnp.inf)
    mx = jnp.max(s, axis=-1, keepdims=True)
    p = jnp.exp(s - mx)
    den = jnp.sum(p, axis=-1)
    o = jnp.einsum('bhtm,btmhd->bthd', p, vg) / den.transpose(0, 2, 1)[..., None]
    lse = (mx[..., 0] + jnp.log(den)).transpose(0, 2, 1)
    return o, lse


def merge_groups(outs, lses, w_o, dtype):
    wts = jax.nn.softmax(jnp.stack(lses, axis=0), axis=0)
    o = jnp.sum(wts[..., None] * jnp.stack(outs, axis=0), axis=0)
    B, L = o.shape[:2]
    return o.reshape(B, L, DSA_WIDTH).astype(dtype) @ w_o


def dsa_prompt(x, w_qkv, q_gain, k_gain, w_o):
    B, L, _ = x.shape
    q, k, v = dsa_project(x, w_qkv, q_gain, k_gain, jnp.arange(L))
    outs, lses, bufs = [], [], []
    for g, (window, dil) in enumerate(DSA_GROUPS):
        o, lse = dilated_band_attention(q[:, :, g], k[:, :, g], v[:, :, g], window, dil)
        outs.append(o)
        lses.append(lse)
        keep = min(window, L)
        bufs.append(jnp.stack([k[:, L - keep:, g], v[:, L - keep:, g]], axis=2))
    return merge_groups(outs, lses, w_o, x.dtype), bufs


def dsa_sample(x, bufs_in, w_qkv, q_gain, k_gain, w_o):
    B, T, _ = x.shape
    q, k, v = dsa_project(x, w_qkv, q_gain, k_gain, PAST_LEN + jnp.arange(T))
    outs, lses, bufs = [], [], []
    for g, (window, dil) in enumerate(DSA_GROUPS):
        buf = bufs_in[g]
        n_buf = buf.shape[1]
        kk = jnp.concatenate([buf[:, :, 0], k[:, :, g]], axis=1)
        vv = jnp.concatenate([buf[:, :, 1], v[:, :, g]], axis=1)
        o, lse = dilated_gather_attention(q[:, :, g], kk, vv, window, dil, n_buf)
        outs.append(o)
        lses.append(lse)
        bufs.append(jnp.stack([kk[:, -n_buf:], vv[:, -n_buf:]], axis=2))
    return merge_groups(outs, lses, w_o, x.dtype), bufs


def gla_chunked(q, k, v, log_a, s0):
    B, L, H, _ = q.shape
    C = GLA_CHUNK
    Lp = -(-L // C) * C
    nc = Lp // C

    def chunks(t):
        t = jnp.pad(t, ((0, 0), (0, Lp - L), (0, 0), (0, 0)))
        return t.reshape(B, nc, C, H, t.shape[-1]).transpose(1, 0, 3, 2, 4)

    causal = jnp.tril(jnp.ones((C, C), dtype=bool))

    def step(s, inp):
        qc, kc, vc, gc = inp
        b = jnp.cumsum(gc, axis=2)
        diff = b[:, :, :, None, :] - b[:, :, None, :, :]
        decay = jnp.exp(jnp.where(causal[:, :, None], diff, -jnp.inf))
        a = jnp.einsum('bhijd,bhjd->bhij', qc[:, :, :, None, :] * decay, kc)
        o = jnp.einsum('bhij,bhje->bhie', a, vc) + jnp.einsum('bhid,bhde->bhie', qc * jnp.exp(b), s)
        b_end = b[:, :, -1:, :]
        s_new = jnp.exp(b_end[:, :, 0, :, None]) * s + jnp.einsum('bhjd,bhje->bhde', kc * jnp.exp(b_end - b), vc)
        return s_new, o

    s, o = lax.scan(step, s0, (chunks(q), chunks(k), chunks(v), chunks(log_a)))
    o = o.transpose(1, 0, 3, 2, 4).reshape(B, Lp, H, -1)[:, :L]
    return o, s


def gla_mixer(x, s0, w_in, w_gate2, b_gate, norm_gain, w_o):
    B, L, _ = x.shape
    h = x @ w_in
    q, k, v, r, gl = jnp.split(h, [GLA_KD, 2 * GLA_KD, 2 * GLA_KD + GLA_VD, 2 * GLA_KD + 2 * GLA_VD], axis=-1)
    log_a = jax.nn.log_sigmoid((gl @ w_gate2 + b_gate).astype(F32)) / GLA_GATE_TAU

    def heads(t, d):
        return t.astype(F32).reshape(B, L, GLA_HEADS, d)

    o, s = gla_chunked(heads(q, GLA_DK) * (GLA_DK ** -0.5), heads(k, GLA_DK), heads(v, GLA_DV),
                       log_a.reshape(B, L, GLA_HEADS, GLA_DK), s0.astype(F32))
    o = rms_norm(o, norm_gain) * jax.nn.silu(heads(r, GLA_DV))
    return o.reshape(B, L, GLA_VD).astype(x.dtype) @ w_o, s


def conv_ffn(x, conv_state, w_in, conv_w, conv_b, w_out):
    L = x.shape[1]
    g, u = jnp.split(x @ w_in, [D_FF], axis=-1)
    gp = jnp.concatenate([conv_state.astype(g.dtype), g], axis=1)
    c = conv_b
    for j in range(CONV_W):
        c = c + conv_w[j] * gp[:, j:j + L]
    y = (jax.nn.silu(c) * u) @ w_out
    return y, gp[:, L:]


def setup_inputs(seed: int = 0) -> dict:
    key = jax.random.key(seed)
    ks = jax.random.split(key, 24)

    def nrm(k, shape, scale):
        return jax.random.normal(k, shape, F32) * scale

    out_scale = (2 * DEPTH) ** -0.5
    bufs = [min(w, PAST_LEN) for w, _ in DSA_GROUPS]
    return {
        'x_prompt': nrm(ks[0], (BATCH, SEQ, D_MODEL), 1.0),
        'x_sample': nrm(ks[1], (DEC_BATCH, DEC_SEQ, D_MODEL), 1.0),
        'cache_kv_w128': nrm(ks[2], (N_DSA_LAYERS, DEC_BATCH, bufs[0], 2, DSA_HEADS, HEAD_DIM), 1.0),
        'cache_kv_w512': nrm(ks[3], (N_DSA_LAYERS, DEC_BATCH, bufs[1], 2, DSA_HEADS, HEAD_DIM), 1.0),
        'cache_kv_w2048': nrm(ks[4], (N_DSA_LAYERS, DEC_BATCH, bufs[2], 2, DSA_HEADS, HEAD_DIM), 1.0),
        'state_gla': nrm(ks[5], (N_GLA_LAYERS, DEC_BATCH, GLA_HEADS, GLA_DK, GLA_DV), 1.0),
        'state_ffn_conv': nrm(ks[6], (DEPTH, DEC_BATCH, CONV_W - 1, D_FF), 1.0),
        'norm_mix': 1.0 + nrm(ks[7], (DEPTH, D_MODEL), 0.02),
        'norm_ffn': 1.0 + nrm(ks[8], (DEPTH, D_MODEL), 0.02),
        'dsa_w_qkv': nrm(ks[9], (N_DSA_LAYERS, D_MODEL, N_GROUPS * 3 * DSA_WIDTH), D_MODEL ** -0.5),
        'dsa_q_gain': 1.0 + nrm(ks[10], (N_DSA_LAYERS, N_GROUPS, HEAD_DIM), 0.02),
        'dsa_k_gain': 1.0 + nrm(ks[11], (N_DSA_LAYERS, N_GROUPS, HEAD_DIM), 0.02),
        'dsa_w_o': nrm(ks[12], (N_DSA_LAYERS, DSA_WIDTH, D_MODEL), DSA_WIDTH ** -0.5 * out_scale),
        'gla_w_in': nrm(ks[13], (N_GLA_LAYERS, D_MODEL, GLA_IN), D_MODEL ** -0.5),
        'gla_w_gate2': nrm(ks[14], (N_GLA_LAYERS, GLA_GATE_RANK, GLA_KD), GLA_GATE_RANK ** -0.5),
        'gla_b_gate': nrm(ks[15], (N_GLA_LAYERS, GLA_KD), 0.1),
        'gla_norm_gain': 1.0 + nrm(ks[16], (N_GLA_LAYERS, GLA_DV), 0.02),
        'gla_w_o': nrm(ks[17], (N_GLA_LAYERS, GLA_VD, D_MODEL), GLA_VD ** -0.5 * out_scale),
        'ffn_w_in': nrm(ks[18], (DEPTH, D_MODEL, 2 * D_FF), D_MODEL ** -0.5),
        'ffn_conv_w': nrm(ks[19], (DEPTH, CONV_W, D_FF), CONV_W ** -0.5),
        'ffn_conv_b': nrm(ks[20], (DEPTH, D_FF), 0.01),
        'ffn_w_out': nrm(ks[21], (DEPTH, D_FF, D_MODEL), D_FF ** -0.5 * out_scale),
    }


def reference(x_prompt, x_sample, cache_kv_w128, cache_kv_w512, cache_kv_w2048, state_gla, state_ffn_conv,
              norm_mix, norm_ffn, dsa_w_qkv, dsa_q_gain, dsa_k_gain, dsa_w_o,
              gla_w_in, gla_w_gate2, gla_b_gate, gla_norm_gain, gla_w_o,
              ffn_w_in, ffn_conv_w, ffn_conv_b, ffn_w_out):
    caches = (cache_kv_w128, cache_kv_w512, cache_kv_w2048)
    xp, xs = x_prompt, x_sample
    kv_p = [[] for _ in DSA_GROUPS]
    kv_s = [[] for _ in DSA_GROUPS]
    gla_p, gla_s, conv_p, conv_s = [], [], [], []
    for i in range(DEPTH):
        j = i // N_MIXERS
        hp = rms_norm(xp, norm_mix[i])
        hs = rms_norm(xs, norm_mix[i])
        if i % N_MIXERS == 0:
            s0 = jnp.zeros((xp.shape[0], GLA_HEADS, GLA_DK, GLA_DV), F32)
            mp, sp = gla_mixer(hp, s0, gla_w_in[j], gla_w_gate2[j], gla_b_gate[j], gla_norm_gain[j], gla_w_o[j])
            ms, ss = gla_mixer(hs, state_gla[j], gla_w_in[j], gla_w_gate2[j], gla_b_gate[j], gla_norm_gain[j], gla_w_o[j])
            gla_p.append(sp)
            gla_s.append(ss)
        else:
            mp, bp = dsa_prompt(hp, dsa_w_qkv[j], dsa_q_gain[j], dsa_k_gain[j], dsa_w_o[j])
            ms, bs = dsa_sample(hs, [c[j] for c in caches], dsa_w_qkv[j], dsa_q_gain[j], dsa_k_gain[j], dsa_w_o[j])
            for g in range(N_GROUPS):
                kv_p[g].append(bp[g])
                kv_s[g].append(bs[g])
        xp = xp + mp
        xs = xs + ms
        fp = rms_norm(xp, norm_ffn[i])
        fs = rms_norm(xs, norm_ffn[i])
        zp = jnp.zeros((xp.shape[0], CONV_W - 1, D_FF), xp.dtype)
        yp, cp = conv_ffn(fp, zp, ffn_w_in[i], ffn_conv_w[i], ffn_conv_b[i], ffn_w_out[i])
        ys, cs = conv_ffn(fs, state_ffn_conv[i], ffn_w_in[i], ffn_conv_w[i], ffn_conv_b[i], ffn_w_out[i])
        conv_p.append(cp)
        conv_s.append(cs)
        xp = xp + yp
        xs = xs + ys
    kv128_p, kv512_p, kv2048_p = [jnp.stack(b, axis=0) for b in kv_p]
    kv128_s, kv512_s, kv2048_s = [jnp.stack(b, axis=0) for b in kv_s]
    new_gla_p = jnp.stack(gla_p, axis=0)
    new_gla_s = jnp.stack(gla_s, axis=0)
    new_conv_p = jnp.stack(conv_p, axis=0)
    new_conv_s = jnp.stack(conv_s, axis=0)
    return (xp, xs, kv128_p, kv128_s, kv512_p, kv512_s, kv2048_p, kv2048_s, new_gla_p, new_gla_s, new_conv_p, new_conv_s)
```

```python
import functools
import math

import numpy as np
import jax
import jax.numpy as jnp
from jax import lax
from jax.experimental import pallas as pl
from jax.experimental.pallas import tpu as pltpu

F32 = jnp.float32
BF16 = jnp.bfloat16

EPS = 1e-6
HEAD_DIM = 128
ROT_DIM = HEAD_DIM // 4
ROPE_THETA = 500000.0
DSA_GROUPS = ((128, 1), (512, 4), (2048, 16))
N_GROUPS = len(DSA_GROUPS)
GLA_GATE_RANK = 16
GLA_GATE_TAU = 16.0
CONV_W = 3
PAST_LEN = 8192
LANES = 128
SAMPLE_ROWS = 16
VMEM_LIMIT_BYTES = 56 * 1024 * 1024


def _params(*sem):
    return pltpu.CompilerParams(dimension_semantics=sem, vmem_limit_bytes=VMEM_LIMIT_BYTES)


def _rms_kernel(x_ref, g_ref, o_ref):
    x = x_ref[...]
    y = x * lax.rsqrt(jnp.mean(x * x, axis=-1, keepdims=True) + EPS)
    o_ref[...] = (y * g_ref[...]).astype(o_ref.dtype)


def rms_bf16(x, gains, layer, tm):
    M, D = x.shape
    return pl.pallas_call(
        _rms_kernel,
        grid=(M // tm,),
        in_specs=[pl.BlockSpec((tm, D), lambda i: (i, 0)),
                  pl.BlockSpec((None, 1, D), lambda i: (layer, 0, 0))],
        out_specs=pl.BlockSpec((tm, D), lambda i: (i, 0)),
        out_shape=jax.ShapeDtypeStruct((M, D), BF16),
        compiler_params=_params("parallel"),
        name="rms_bf16",
    )(x, gains)


def _mm_wres_kernel(a_ref, w_ref, *rest, has_res, epilogue):
    n_extra = len(rest) - 2 - (1 if has_res else 0)
    extra = rest[:n_extra]
    rest = rest[n_extra:]
    if has_res:
        r_ref, o_ref, wb_ref = rest
    else:
        o_ref, wb_ref = rest
    i = pl.program_id(1)

    @pl.when(i == 0)
    def _():
        wb_ref[...] = w_ref[...].astype(BF16)

    acc = jnp.dot(a_ref[...], wb_ref[...], preferred_element_type=F32)
    if has_res:
        acc = acc + r_ref[...]
    if epilogue is None:
        o_ref[...] = acc.astype(o_ref.dtype)
    else:
        epilogue(acc, o_ref, *extra)


def mm_wres(a, w, layer, *, tm, tn, col0=0, ncols=None, out_dtype=F32, residual=None,
            epilogue=None, extra=(), extra_specs=(), name="mm_wres"):
    M, K = a.shape
    ncols = w.shape[2] - col0 if ncols is None else ncols
    cb0 = col0 // tn
    in_specs = [pl.BlockSpec((tm, K), lambda j, i: (i, 0)),
                pl.BlockSpec((None, K, tn), lambda j, i: (layer, 0, cb0 + j))]
    in_specs += list(extra_specs)
    args = [a, w, *extra]
    if residual is not None:
        in_specs.append(pl.BlockSpec((tm, tn), lambda j, i: (i, j)))
        args.append(residual)
    kern = functools.partial(_mm_wres_kernel, has_res=residual is not None, epilogue=epilogue)
    return pl.pallas_call(
        kern,
        grid=(ncols // tn, M // tm),
        in_specs=in_specs,
        out_specs=pl.BlockSpec((tm, tn), lambda j, i: (i, j)),
        out_shape=jax.ShapeDtypeStruct((M, ncols), out_dtype),
        scratch_shapes=[pltpu.VMEM((K, tn), BF16)],
        compiler_params=_params("parallel", "arbitrary"),
        name=name,
    )(*args)


def _mm_kt_kernel(a_ref, w_ref, r_ref, o_ref):
    k = pl.program_id(2)

    @pl.when(k == 0)
    def _():
        o_ref[...] = r_ref[...]

    o_ref[...] += jnp.dot(a_ref[...], w_ref[...].astype(BF16), preferred_element_type=F32)


def mm_ktiled(a, w, layer, residual, *, tm, tn, tk, name="mm_ktiled"):
    M, K = a.shape
    N = w.shape[2]
    return pl.pallas_call(
        _mm_kt_kernel,
        grid=(M // tm, N // tn, K // tk),
        in_specs=[pl.BlockSpec((tm, tk), lambda i, j, k: (i, k)),
                  pl.BlockSpec((None, tk, tn), lambda i, j, k: (layer, k, j)),
                  pl.BlockSpec((tm, tn), lambda i, j, k: (i, j))],
        out_specs=pl.BlockSpec((tm, tn), lambda i, j, k: (i, j)),
        out_shape=jax.ShapeDtypeStruct((M, N), F32),
        compiler_params=_params("parallel", "parallel", "arbitrary"),
        name=name,
    )(a, w, residual)


def _silu(x):
    return x * jax.nn.sigmoid(x)


def _ffn_in_prompt_kernel(a_ref, wg_ref, wu_ref, cw_ref, cb_ref, act_ref, st_ref,
                          wgb_ref, wub_ref, carry_ref, *, tiles_per_seq):
    i = pl.program_id(1)

    @pl.when(i == 0)
    def _():
        wgb_ref[...] = wg_ref[...].astype(BF16)
        wub_ref[...] = wu_ref[...].astype(BF16)

    @pl.when(i % tiles_per_seq == 0)
    def _():
        carry_ref[...] = jnp.zeros_like(carry_ref)

    a = a_ref[...]
    g = jnp.dot(a, wgb_ref[...], preferred_element_type=F32)
    u = jnp.dot(a, wub_ref[...], preferred_element_type=F32)
    tm = g.shape[0]
    prev = carry_ref[...]
    row = lax.broadcasted_iota(jnp.int32, g.shape, 0)
    g1 = jnp.where(row == 0, prev[7:8], pltpu.roll(g, 1, axis=0))
    g2 = jnp.where(row == 0, prev[6:7],
                   jnp.where(row == 1, prev[7:8], pltpu.roll(g, 2, axis=0)))
    cw = cw_ref[...]
    c = cb_ref[...] + cw[0:1] * g2 + cw[1:2] * g1 + cw[2:3] * g
    act_ref[...] = (_silu(c) * u).astype(act_ref.dtype)
    carry_ref[...] = g[tm - 8:tm]
    st_ref[...] = g[tm - (CONV_W - 1):tm]


def ffn_in_prompt(a, w_in, conv_w, conv_b, layer, *, seq, tm, tw):
    M, K = a.shape
    d_ff = w_in.shape[2] // 2
    nb = M // seq
    tps = seq // tm
    ub0 = d_ff // tw
    kern = functools.partial(_ffn_in_prompt_kernel, tiles_per_seq=tps)
    return pl.pallas_call(
        kern,
        grid=(d_ff // tw, M // tm),
        in_specs=[pl.BlockSpec((tm, K), lambda j, i: (i, 0)),
                  pl.BlockSpec((None, K, tw), lambda j, i: (layer, 0, j)),
                  pl.BlockSpec((None, K, tw), lambda j, i: (layer, 0, ub0 + j)),
                  pl.BlockSpec((None, CONV_W, tw), lambda j, i: (layer, 0, j)),
                  pl.BlockSpec((None, 1, tw), lambda j, i: (layer, 0, j))],
        out_specs=[pl.BlockSpec((tm, tw), lambda j, i: (i, j)),
                   pl.BlockSpec((None, CONV_W - 1, tw), lambda j, i: (i // tps, 0, j))],
        out_shape=[jax.ShapeDtypeStruct((M, d_ff), BF16),
                   jax.ShapeDtypeStruct((nb, CONV_W - 1, d_ff), F32)],
        scratch_shapes=[pltpu.VMEM((K, tw), BF16), pltpu.VMEM((K, tw), BF16),
                        pltpu.VMEM((8, tw), F32)],
        compiler_params=_params("parallel", "arbitrary"),
        name="ffn_in_prompt",
    )(a, w_in, w_in, conv_w, conv_b)


def _ffn_in_sample_kernel(a_ref, wg_ref, wu_ref, cw_ref, cb_ref, s0_ref, s1_ref, act_ref, g_ref):
    a = a_ref[...]
    g = jnp.dot(a, wg_ref[...].astype(BF16), preferred_element_type=F32)
    u = jnp.dot(a, wu_ref[...].astype(BF16), preferred_element_type=F32)
    cw = cw_ref[...]
    c = cb_ref[...] + cw[0:1] * s0_ref[...] + cw[1:2] * s1_ref[...] + cw[2:3] * g
    act_ref[...] = (_silu(c) * u).astype(act_ref.dtype)
    g_ref[...] = g


def ffn_in_sample(a, w_in, conv_w, conv_b, st0, st1, layer, *, tw):
    R, K = a.shape
    d_ff = w_in.shape[2] // 2
    ub0 = d_ff // tw
    return pl.pallas_call(
        _ffn_in_sample_kernel,
        grid=(d_ff // tw,),
        in_specs=[pl.BlockSpec((R, K), lambda j: (0, 0)),
                  pl.BlockSpec((None, K, tw), lambda j: (layer, 0, j)),
                  pl.BlockSpec((None, K, tw), lambda j: (layer, 0, ub0 + j)),
                  pl.BlockSpec((None, CONV_W, tw), lambda j: (layer, 0, j)),
                  pl.BlockSpec((None, 1, tw), lambda j: (layer, 0, j)),
                  pl.BlockSpec((R, tw), lambda j: (0, j)),
                  pl.BlockSpec((R, tw), lambda j: (0, j))],
        out_specs=[pl.BlockSpec((R, tw), lambda j: (0, j)),
                   pl.BlockSpec((R, tw), lambda j: (0, j))],
        out_shape=[jax.ShapeDtypeStruct((R, d_ff), BF16),
                   jax.ShapeDtypeStruct((R, d_ff), F32)],
        compiler_params=_params("parallel"),
        name="ffn_in_sample",
    )(a, w_in, w_in, conv_w, conv_b, st0, st1)


def _gla_tables(C):
    n_levels = int(math.log2(C))
    assert 1 << n_levels == C
    i = np.arange(C)[:, None]
    t = np.arange(C)[None, :]
    mats = [t <= i, t > i]
    level = np.full((C, C), -1, np.int32)
    level[np.arange(C), np.arange(C)] = n_levels
    for lv in range(n_levels):
        s = 1 << lv
        mid = (i // (2 * s)) * (2 * s) + s
        upper = i >= mid
        mats.append((upper & (t >= mid) & (t <= i)) | (~upper & (t > i) & (t < mid)))
        same = (i // (2 * s)) == (t // (2 * s))
        own = same & upper & ~((t % (2 * s)) >= s)
        level[own] = lv
    return np.concatenate(mats, axis=0).astype(np.float32), level, n_levels


def _nt(a, b):
    return lax.dot_general(a, b, (((1,), (1,)), ((), ())), preferred_element_type=F32)


def _tn(a, b):
    return lax.dot_general(a, b, (((0,), (0,)), ((), ())), preferred_element_type=F32)


def _gla_kernel(q_ref, k_ref, v_ref, r_ref, gl_ref, w2_ref, bg_ref, ng_ref, s0_ref, ms_ref, lvl_ref,
                o_ref, sout_ref, st_ref, *, C, n_levels, seq_len, dk):
    c = pl.program_id(2)
    nc = pl.num_programs(2)

    @pl.when(c == 0)
    def _():
        st_ref[...] = s0_ref[...].T

    z = jnp.dot(gl_ref[...].astype(BF16), w2_ref[...].astype(BF16), preferred_element_type=F32) + bg_ref[...]
    g = -(jnp.maximum(-z, 0.0) + jnp.log1p(jnp.exp(-jnp.abs(z)))) / GLA_GATE_TAU
    if seq_len % C:
        row = c * C + lax.broadcasted_iota(jnp.int32, g.shape, 0)
        g = jnp.where(row < seq_len, g, 0.0)
    g_hi = g.astype(BF16)
    g_lo = (g - g_hi.astype(F32)).astype(BF16)
    ms = ms_ref[...]
    ex = jnp.dot(ms, g_hi, preferred_element_type=F32) + jnp.dot(ms, g_lo, preferred_element_type=F32)
    e = jnp.exp(ex)

    q = q_ref[...] * (dk ** -0.5)
    k = k_ref[...]
    vb = v_ref[...].astype(BF16)
    e_b = e[0:C]
    e_end = e[C:2 * C]
    st = st_ref[...]
    o = _nt((q * e_b).astype(BF16), st.astype(BF16))
    lvl = lvl_ref[...]
    a = jnp.where(lvl == n_levels, _nt(q.astype(BF16), k.astype(BF16)), 0.0)
    for lv in range(n_levels):
        e_l = e[(2 + lv) * C:(3 + lv) * C]
        a = jnp.where(lvl == lv, _nt((q * e_l).astype(BF16), (k * e_l).astype(BF16)), a)
    o = o + jnp.dot(a.astype(BF16), vb, preferred_element_type=F32)
    decay_end = e_b[C - 1:C]
    st_ref[...] = st * decay_end + _tn(vb, (k * e_end).astype(BF16))

    y = o * lax.rsqrt(jnp.mean(o * o, axis=-1, keepdims=True) + EPS) * ng_ref[...]
    o_ref[...] = (y * _silu(r_ref[...])).astype(o_ref.dtype)

    @pl.when(c == nc - 1)
    def _():
        sout_ref[...] = st_ref[...].T


def gla_core(h, gl, w_gate2, b_gate, norm_gain, s0, layer, s0_layer, *, seq_len, C, n_heads, dk, dv):
    B, Lp, _ = h.shape
    ms_np, lvl_np, n_levels = _gla_tables(C)
    ms = jnp.asarray(ms_np, BF16)
    lvl = jnp.asarray(lvl_np)
    kd_blocks = n_heads
    v0 = 2 * n_heads * dk // dv
    r0 = v0 + n_heads
    kern = functools.partial(_gla_kernel, C=C, n_levels=n_levels, seq_len=seq_len, dk=dk)
    R = ms_np.shape[0]
    return pl.pallas_call(
        kern,
        grid=(B, n_heads, Lp // C),
        in_specs=[pl.BlockSpec((None, C, dk), lambda b, hd, c: (b, c, hd)),
                  pl.BlockSpec((None, C, dk), lambda b, hd, c: (b, c, kd_blocks + hd)),
                  pl.BlockSpec((None, C, dv), lambda b, hd, c: (b, c, v0 + hd)),
                  pl.BlockSpec((None, C, dv), lambda b, hd, c: (b, c, r0 + hd)),
                  pl.BlockSpec((None, C, LANES), lambda b, hd, c: (b, c, 0)),
                  pl.BlockSpec((None, LANES, dk), lambda b, hd, c: (layer, 0, hd)),
                  pl.BlockSpec((None, 1, dk), lambda b, hd, c: (layer, 0, hd)),
                  pl.BlockSpec((None, 1, dv), lambda b, hd, c: (layer, 0, 0)),
                  pl.BlockSpec((None, None, None, dk, dv), lambda b, hd, c: (s0_layer, b, hd, 0, 0)),
                  pl.BlockSpec((R, C), lambda b, hd, c: (0, 0)),
                  pl.BlockSpec((C, C), lambda b, hd, c: (0, 0))],
        out_specs=[pl.BlockSpec((None, C, dv), lambda b, hd, c: (b, c, hd)),
                   pl.BlockSpec((None, None, dk, dv), lambda b, hd, c: (b, hd, 0, 0))],
        out_shape=[jax.ShapeDtypeStruct((B, Lp, n_heads * dv), BF16),
                   jax.ShapeDtypeStruct((B, n_heads, dk, dv), F32)],
        scratch_shapes=[pltpu.VMEM((dv, dk), F32)],
        compiler_params=_params("parallel", "parallel", "arbitrary"),
        name="gla_core",
    )(h, h, h, h, gl, w_gate2, b_gate, norm_gain, s0, ms, lvl)


def _rope_tables(pos):
    half = ROT_DIM // 2
    inv = jnp.power(ROPE_THETA, -jnp.arange(half, dtype=F32) * (2.0 / ROT_DIM))
    ang = pos.astype(F32)[:, None] * inv[None, :]
    cos, sin = jnp.cos(ang), jnp.sin(ang)
    n = pos.shape[0]
    cos_t = jnp.concatenate([cos, cos, jnp.ones((n, HEAD_DIM - ROT_DIM), F32)], axis=1)
    sin_t = jnp.concatenate([-sin, sin, jnp.zeros((n, HEAD_DIM - ROT_DIM), F32)], axis=1)
    return cos_t, sin_t


def _qkv_epilogue(acc, o_ref, gain_ref, cos_ref, sin_ref, *, tn, section):
    j = pl.program_id(0)
    kind = (j * tn // section) % 3

    @pl.when(kind == 2)
    def _():
        o_ref[...] = acc

    @pl.when(kind != 2)
    def _():
        half = ROT_DIM // 2
        cos_t = cos_ref[...]
        sin_t = sin_ref[...]
        gain = gain_ref[...]
        lane = lax.broadcasted_iota(jnp.int32, cos_t.shape, 1)
        for hh in range(tn // HEAD_DIM):
            x = acc[:, hh * HEAD_DIM:(hh + 1) * HEAD_DIM]
            y = x * lax.rsqrt(jnp.mean(x * x, axis=-1, keepdims=True) + EPS) * gain
            swapped = jnp.where(lane < half, pltpu.roll(y, HEAD_DIM - half, axis=1), pltpu.roll(y, half, axis=1))
            o_ref[:, hh * HEAD_DIM:(hh + 1) * HEAD_DIM] = y * cos_t + swapped * sin_t


def dsa_qkv(a, w_qkv, gains, cos_t, sin_t, layer, *, tm, tn, section):
    n_tab = cos_t.shape[0] // tm
    n_sec = w_qkv.shape[2] // section
    extra_specs = [pl.BlockSpec((None, 1, HEAD_DIM), lambda j, i: (layer * n_sec + j * tn // section, 0, 0)),
                   pl.BlockSpec((tm, HEAD_DIM), lambda j, i: (i % n_tab, 0)),
                   pl.BlockSpec((tm, HEAD_DIM), lambda j, i: (i % n_tab, 0))]
    epi = functools.partial(_qkv_epilogue, tn=tn, section=section)
    return mm_wres(a, w_qkv, layer, tm=tm, tn=tn, epilogue=epi, extra=(gains, cos_t, sin_t),
                   extra_specs=extra_specs, name="dsa_qkv")


def _band_attn_kernel(*refs, n_heads, has_prev):
    if has_prev:
        q_ref, kc_ref, vc_ref, kp_ref, vp_ref, o_ref, lse_ref = refs
    else:
        q_ref, kc_ref, vc_ref, o_ref, lse_ref = refs
    n = pl.program_id(2)
    blk = q_ref.shape[0]
    qi = lax.broadcasted_iota(jnp.int32, (blk, blk), 0)
    ki = lax.broadcasted_iota(jnp.int32, (blk, blk), 1)
    cur_ok = ki <= qi
    prev_ok = (ki >= qi) & (n > 0)
    lane = lax.broadcasted_iota(jnp.int32, (blk, LANES), 1)
    scale = HEAD_DIM ** -0.5
    lse_all = jnp.zeros((blk, LANES), F32)
    for hh in range(n_heads):
        sl = slice(hh * HEAD_DIM, (hh + 1) * HEAD_DIM)
        qh = q_ref[:, sl].astype(BF16)
        s_c = jnp.where(cur_ok, _nt(qh, kc_ref[:, sl].astype(BF16)) * scale, -jnp.inf)
        mx = jnp.max(s_c, axis=-1, keepdims=True)
        if has_prev:
            s_p = jnp.where(prev_ok, _nt(qh, kp_ref[:, sl].astype(BF16)) * scale, -jnp.inf)
            mx = jnp.maximum(mx, jnp.max(s_p, axis=-1, keepdims=True))
        p_c = jnp.exp(s_c - mx)
        den = jnp.sum(p_c, axis=-1, keepdims=True)
        o = jnp.dot(p_c.astype(BF16), vc_ref[:, sl].astype(BF16), preferred_element_type=F32)
        if has_prev:
            p_p = jnp.exp(s_p - mx)
            den = den + jnp.sum(p_p, axis=-1, keepdims=True)
            o = o + jnp.dot(p_p.astype(BF16), vp_ref[:, sl].astype(BF16), preferred_element_type=F32)
        o_ref[:, sl] = o / den
        lse_all = jnp.where(lane == hh, mx + jnp.log(den), lse_all)
    lse_ref[...] = lse_all


def band_attention(qkv, group, *, window, dil, n_heads):
    B, S, tot = qkv.shape
    W = n_heads * HEAD_DIM
    blk = window // dil
    nb = S // (blk * dil)
    assert nb * blk * dil == S
    per_res = tot // W
    qv = qkv.reshape(B, S // dil, dil * tot)
    c0 = group * 3
    has_prev = nb > 1
    kern = functools.partial(_band_attn_kernel, n_heads=n_heads, has_prev=has_prev)

    def spec(off, prev):
        if prev:
            return pl.BlockSpec((None, blk, W), lambda b, r, n: (b, jnp.maximum(n - 1, 0), r * per_res + c0 + off))
        return pl.BlockSpec((None, blk, W), lambda b, r, n: (b, n, r * per_res + c0 + off))

    in_specs = [spec(0, False), spec(1, False), spec(2, False)]
    if has_prev:
        in_specs += [spec(1, True), spec(2, True)]
    o, lse = pl.pallas_call(
        kern,
        grid=(B, dil, nb),
        in_specs=in_specs,
        out_specs=[pl.BlockSpec((None, blk, W), lambda b, r, n: (b, n, r)),
                   pl.BlockSpec((None, blk, LANES), lambda b, r, n: (b, n, r))],
        out_shape=[jax.ShapeDtypeStruct((B, S // dil, dil * W), F32),
                   jax.ShapeDtypeStruct((B, S // dil, dil * LANES), F32)],
        compiler_params=_params("parallel", "parallel", "parallel"),
        name=f"band_attention_g{group}",
    )(*([qv] * len(in_specs)))
    return o.reshape(B, S, W), lse.reshape(B, S, LANES)


def _merge_kernel(o0_ref, o1_ref, o2_ref, l0_ref, l1_ref, l2_ref, out_ref, *, n_heads):
    l0, l1, l2 = l0_ref[...], l1_ref[...], l2_ref[...]
    mx = jnp.maximum(jnp.maximum(l0, l1), l2)
    e0, e1, e2 = jnp.exp(l0 - mx), jnp.exp(l1 - mx), jnp.exp(l2 - mx)
    den = e0 + e1 + e2
    w0, w1, w2 = e0 / den, e1 / den, e2 / den
    for hh in range(n_heads):
        sl = slice(hh * HEAD_DIM, (hh + 1) * HEAD_DIM)
        o = (w0[:, hh:hh + 1] * o0_ref[:, sl] + w1[:, hh:hh + 1] * o1_ref[:, sl]
             + w2[:, hh:hh + 1] * o2_ref[:, sl])
        out_ref[:, sl] = o.astype(out_ref.dtype)


def merge_groups(outs, lses, *, tm, n_heads):
    M, W = outs[0].shape
    return pl.pallas_call(
        functools.partial(_merge_kernel, n_heads=n_heads),
        grid=(M // tm,),
        in_specs=[pl.BlockSpec((tm, W), lambda i: (i, 0))] * 3 + [pl.BlockSpec((tm, LANES), lambda i: (i, 0))] * 3,
        out_specs=pl.BlockSpec((tm, W), lambda i: (i, 0)),
        out_shape=jax.ShapeDtypeStruct((M, W), BF16),
        compiler_params=_params("parallel"),
        name="merge_groups",
    )(*outs, *lses)


def _sample_attn_kernel(qkv_ref, c0_ref, c1_ref, c2_ref, out_ref, *, n_heads):
    W = n_heads * HEAD_DIM
    scale = HEAD_DIM ** -0.5
    rows = 8
    for hh in range(n_heads):
        os, ls = [], []
        for g, c_ref in enumerate((c0_ref, c1_ref, c2_ref)):
            base = g * 3 * W + hh * HEAD_DIM
            q = qkv_ref[:, base:base + HEAD_DIM]
            k_new = qkv_ref[:, base + W:base + W + HEAD_DIM]
            v_new = qkv_ref[:, base + 2 * W:base + 2 * W + HEAD_DIM]
            qb = jnp.broadcast_to(q, (rows, HEAD_DIM)).astype(BF16)
            kc = c_ref[:, hh * HEAD_DIM:(hh + 1) * HEAD_DIM].astype(BF16)
            vc = c_ref[:, W + hh * HEAD_DIM:W + (hh + 1) * HEAD_DIM].astype(BF16)
            s_c = _nt(qb, kc)[0:1] * scale
            s_n = jnp.sum(q.astype(BF16).astype(F32) * k_new.astype(BF16).astype(F32), axis=-1, keepdims=True) * scale
            mx = jnp.maximum(jnp.max(s_c, axis=-1, keepdims=True), s_n)
            p_c = jnp.exp(s_c - mx)
            p_n = jnp.exp(s_n - mx)
            den = jnp.sum(p_c, axis=-1, keepdims=True) + p_n
            pb = jnp.broadcast_to(p_c, (rows, p_c.shape[1])).astype(BF16)
            o = jnp.dot(pb, vc, preferred_element_type=F32)[0:1]
            o = o + p_n.astype(BF16).astype(F32) * v_new.astype(BF16).astype(F32)
            os.append(o / den)
            ls.append(mx + jnp.log(den))
        mx = jnp.maximum(jnp.maximum(ls[0], ls[1]), ls[2])
        es = [jnp.exp(l - mx) for l in ls]
        den = es[0] + es[1] + es[2]
        o = (es[0] * os[0] + es[1] * os[1] + es[2] * os[2]) / den
        out_ref[:, hh * HEAD_DIM:(hh + 1) * HEAD_DIM] = o.astype(out_ref.dtype)


def sample_attention(qkv, caches, layer, *, n_heads):
    B = qkv.shape[0]
    W = n_heads * HEAD_DIM
    views, specs = [], []
    for (window, dil), c in zip(DSA_GROUPS, caches):
        n_buf = c.shape[2]
        assert n_buf == window and n_buf % dil == 0
        views.append(c.reshape(c.shape[0], B, n_buf // dil, dil * 2 * W))
        specs.append(pl.BlockSpec((None, None, n_buf // dil, 2 * W), lambda b: (layer, b, 0, 0)))
    return pl.pallas_call(
        functools.partial(_sample_attn_kernel, n_heads=n_heads),
        grid=(B,),
        in_specs=[pl.BlockSpec((None, 1, qkv.shape[2]), lambda b: (b, 0, 0))] + specs,
        out_specs=pl.BlockSpec((None, 1, W), lambda b: (b, 0, 0)),
        out_shape=jax.ShapeDtypeStruct((B, 1, W), BF16),
        compiler_params=_params("parallel"),
        name="sample_attention",
    )(qkv, *views)


def kernel(x_prompt, x_sample, cache_kv_w128, cache_kv_w512, cache_kv_w2048, state_gla, state_ffn_conv,
           norm_mix, norm_ffn, dsa_w_qkv, dsa_q_gain, dsa_k_gain, dsa_w_o,
           gla_w_in, gla_w_gate2, gla_b_gate, gla_norm_gain, gla_w_o,
           ffn_w_in, ffn_conv_w, ffn_conv_b, ffn_w_out):
    B, S, D = x_prompt.shape
    BS = x_sample.shape[0]
    assert x_sample.shape[1] == 1
    depth = norm_mix.shape[0]
    caches = (cache_kv_w128, cache_kv_w512, cache_kv_w2048)
    n_gla, gla_heads, gla_dk, gla_dv = state_gla.shape[0], state_gla.shape[2], state_gla.shape[3], state_gla.shape[4]
    gla_kd, gla_vd = gla_heads * gla_dk, gla_heads * gla_dv
    dsa_heads = dsa_w_o.shape[1] // HEAD_DIM
    W = dsa_heads * HEAD_DIM
    d_ff = ffn_w_in.shape[2] // 2
    M = B * S
    R = SAMPLE_ROWS

    tm = min(1024, S)
    tn = min(512, W)
    tw = 256
    tn_out = min(2048, D)
    gla_chunk = min(128, S)

    xp = x_prompt.reshape(M, D)
    xs = jnp.pad(x_sample.reshape(BS, D), ((0, R - BS), (0, 0)))

    norm_mix3 = norm_mix.reshape(depth, 1, D)
    norm_ffn3 = norm_ffn.reshape(depth, 1, D)
    conv_b3 = ffn_conv_b.reshape(depth, 1, d_ff)
    n_main = 2 * gla_kd + 2 * gla_vd
    w_gl = jnp.pad(gla_w_in[:, :, n_main:], ((0, 0), (0, 0), (0, LANES - GLA_GATE_RANK)))
    w_gate2 = jnp.pad(gla_w_gate2, ((0, 0), (0, LANES - GLA_GATE_RANK), (0, 0)))
    b_gate3 = gla_b_gate.reshape(n_gla, 1, gla_kd)
    gla_ng3 = gla_norm_gain.reshape(n_gla, 1, gla_dv)
    ones = jnp.ones_like(dsa_q_gain)
    qk_gains = jnp.stack([dsa_q_gain, dsa_k_gain, ones], axis=2).reshape(-1, 1, HEAD_DIM)
    cos_p, sin_p = _rope_tables(jnp.arange(S))
    cos_s, sin_s = _rope_tables(jnp.full((R,), PAST_LEN, jnp.int32))

    kv_p = [[] for _ in DSA_GROUPS]
    kv_s = [[] for _ in DSA_GROUPS]
    gla_p, gla_s, conv_p, conv_s = [], [], [], []
    zero_state = jnp.zeros((1, B, gla_heads, gla_dk, gla_dv), F32)

    for i in range(depth):
        j = i // 2
        hp = rms_bf16(xp, norm_mix3, i, 256)
        hs = rms_bf16(xs, norm_mix3, i, R)
        if i % 2 == 0:
            h = mm_wres(hp, gla_w_in, j, tm=tm, tn=tn, ncols=n_main, name="gla_in")
            gl = mm_wres(hp, w_gl, j, tm=tm, tn=LANES, name="gla_gate_in")
            o, sp = gla_core(h.reshape(B, S, n_main), gl.reshape(B, S, LANES), w_gate2, b_gate3, gla_ng3,
                             zero_state, j, 0, seq_len=S, C=gla_chunk, n_heads=gla_heads, dk=gla_dk, dv=gla_dv)
            xp = mm_wres(o.reshape(M, gla_vd), gla_w_o, j, tm=tm, tn=tn, residual=xp, name="gla_out")
            h_s = mm_wres(hs, gla_w_in, j, tm=R, tn=tn, ncols=n_main, name="gla_in_s")
            gl_s = mm_wres(hs, w_gl, j, tm=R, tn=LANES, name="gla_gate_in_s")
            h_s = jnp.pad(h_s[:BS].reshape(BS, 1, n_main), ((0, 0), (0, R - 1), (0, 0)))
            gl_s = jnp.pad(gl_s[:BS].reshape(BS, 1, LANES), ((0, 0), (0, R - 1), (0, 0)))
            o_s, ss = gla_core(h_s, gl_s, w_gate2, b_gate3, gla_ng3, state_gla, j, j,
                               seq_len=1, C=R, n_heads=gla_heads, dk=gla_dk, dv=gla_dv)
            o_s = jnp.pad(o_s[:, 0], ((0, R - BS), (0, 0)))
            xs = mm_wres(o_s, gla_w_o, j, tm=R, tn=tn, residual=xs, name="gla_out_s")
            gla_p.append(sp)
            gla_s.append(ss)
        else:
            qkv = dsa_qkv(hp, dsa_w_qkv, qk_gains, cos_p, sin_p, j, tm=tm, tn=tn, section=W)
            qkv3 = qkv.reshape(B, S, N_GROUPS * 3 * W)
            outs, lses = [], []
            for g, (window, dil) in enumerate(DSA_GROUPS):
                og, lg = band_attention(qkv3, g, window=window, dil=dil, n_heads=dsa_heads)
                outs.append(og.reshape(M, W))
                lses.append(lg.reshape(M, LANES))
            merged = merge_groups(outs, lses, tm=512, n_heads=dsa_heads)
            xp = mm_wres(merged, dsa_w_o, j, tm=tm, tn=tn, residual=xp, name="dsa_out")
            qkv6 = qkv.reshape(B, S, N_GROUPS, 3, dsa_heads, HEAD_DIM)
            for g, (window, dil) in enumerate(DSA_GROUPS):
                keep = min(window, S)
                kv_p[g].append(qkv6[:, S - keep:, g, 1:3])
            qkv_s = dsa_qkv(hs, dsa_w_qkv, qk_gains, cos_s, sin_s, j, tm=R, tn=tn, section=W)
            qkv_s = qkv_s[:BS].reshape(BS, 1, N_GROUPS * 3 * W)
            merged_s = sample_attention(qkv_s, caches, j, n_heads=dsa_heads)
            merged_s = jnp.pad(merged_s[:, 0], ((0, R - BS), (0, 0)))
            xs = mm_wres(merged_s, dsa_w_o, j, tm=R, tn=tn, residual=xs, name="dsa_out_s")
            qkv6_s = qkv_s.reshape(BS, 1, N_GROUPS, 3, dsa_heads, HEAD_DIM)
            for g in range(N_GROUPS):
                kv_s[g].append(jnp.concatenate([caches[g][j][:, 1:], qkv6_s[:, :, g, 1:3]], axis=1))
        fp = rms_bf16(xp, norm_ffn3, i, 256)
        act, cp = ffn_in_prompt(fp, ffn_w_in, ffn_conv_w, conv_b3, i, seq=S, tm=tm, tw=tw)
        xp = mm_ktiled(act, ffn_w_out, i, xp, tm=tm, tn=tn_out, tk=tw, name="ffn_out")
        conv_p.append(cp)
        fs = rms_bf16(xs, norm_ffn3, i, R)
        st = jnp.pad(state_ffn_conv[i], ((0, R - BS), (0, 0), (0, 0)))
        act_s, g_s = ffn_in_sample(fs, ffn_w_in, ffn_conv_w, conv_b3, st[:, 0], st[:, 1], i, tw=tw)
        xs = mm_ktiled(act_s, ffn_w_out, i, xs, tm=R, tn=tn_out, tk=tw, name="ffn_out_s")
        conv_s.append(jnp.stack([state_ffn_conv[i][:, 1], g_s[:BS]], axis=1))

    kv128_p, kv512_p, kv2048_p = [jnp.stack(b, axis=0) for b in kv_p]
    kv128_s, kv512_s, kv2048_s = [jnp.stack(b, axis=0) for b in kv_s]
    return (xp.reshape(B, S, D), xs[:BS].reshape(BS, 1, D),
            kv128_p, kv128_s, kv512_p, kv512_s, kv2048_p, kv2048_s,
            jnp.stack(gla_p, axis=0), jnp.stack(gla_s, axis=0),
            jnp.stack(conv_p, axis=0), jnp.stack(conv_s, axis=0))
```

```python
import functools
import math

import numpy as np
import jax
import jax.numpy as jnp
from jax import lax
from jax.experimental import pallas as pl
from jax.experimental.pallas import tpu as pltpu

F32 = jnp.float32
BF16 = jnp.bfloat16

EPS = 1e-6
HEAD_DIM = 128
ROT_DIM = HEAD_DIM // 4
ROPE_THETA = 500000.0
DSA_GROUPS = ((128, 1), (512, 4), (2048, 16))
N_GROUPS = len(DSA_GROUPS)
GLA_GATE_RANK = 16
GLA_GATE_TAU = 16.0
CONV_W = 3
PAST_LEN = 8192
LANES = 128
SAMPLE_ROWS = 16
VMEM_LIMIT_BYTES = 56 * 1024 * 1024


def _params(*sem):
    return pltpu.CompilerParams(dimension_semantics=sem, vmem_limit_bytes=VMEM_LIMIT_BYTES)


def _rms_kernel(x_ref, g_ref, o_ref):
    x = x_ref[...]
    y = x * lax.rsqrt(jnp.mean(x * x, axis=-1, keepdims=True) + EPS)
    o_ref[...] = (y * g_ref[...]).astype(o_ref.dtype)


def rms_bf16(x, gains, layer, tm):
    M, D = x.shape
    return pl.pallas_call(
        _rms_kernel,
        grid=(M // tm,),
        in_specs=[pl.BlockSpec((tm, D), lambda i: (i, 0)),
                  pl.BlockSpec((None, 1, D), lambda i: (layer, 0, 0))],
        out_specs=pl.BlockSpec((tm, D), lambda i: (i, 0)),
        out_shape=jax.ShapeDtypeStruct((M, D), BF16),
        compiler_params=_params("parallel"),
        name="rms_bf16",
    )(x, gains)


def _mm_wres_kernel(a_ref, w_ref, *rest, has_res, epilogue):
    n_extra = len(rest) - 2 - (1 if has_res else 0)
    extra = rest[:n_extra]
    rest = rest[n_extra:]
    if has_res:
        r_ref, o_ref, wb_ref = rest
    else:
        o_ref, wb_ref = rest
    i = pl.program_id(1)

    @pl.when(i == 0)
    def _():
        wb_ref[...] = w_ref[...].astype(BF16)

    acc = jnp.dot(a_ref[...], wb_ref[...], preferred_element_type=F32)
    if has_res:
        acc = acc + r_ref[...]
    if epilogue is None:
        o_ref[...] = acc.astype(o_ref.dtype)
    else:
        epilogue(acc, o_ref, *extra)


def mm_wres(a, w, layer, *, tm, tn, col0=0, ncols=None, out_dtype=F32, residual=None,
            epilogue=None, extra=(), extra_specs=(), out_spec=None, out_shape=None, name="mm_wres"):
    M, K = a.shape
    ncols = w.shape[2] - col0 if ncols is None else ncols
    cb0 = col0 // tn
    if out_spec is None:
        out_spec = pl.BlockSpec((tm, tn), lambda j, i: (i, j))
        out_shape = jax.ShapeDtypeStruct((M, ncols), out_dtype)
    in_specs = [pl.BlockSpec((tm, K), lambda j, i: (i, 0)),
                pl.BlockSpec((None, K, tn), lambda j, i: (layer, 0, cb0 + j))]
    in_specs += list(extra_specs)
    args = [a, w, *extra]
    if residual is not None:
        in_specs.append(pl.BlockSpec((tm, tn), lambda j, i: (i, j)))
        args.append(residual)
    kern = functools.partial(_mm_wres_kernel, has_res=residual is not None, epilogue=epilogue)
    return pl.pallas_call(
        kern,
        grid=(ncols // tn, M // tm),
        in_specs=in_specs,
        out_specs=out_spec,
        out_shape=out_shape,
        scratch_shapes=[pltpu.VMEM((K, tn), BF16)],
        compiler_params=_params("parallel", "arbitrary"),
        name=name,
    )(*args)


def _mm_kt_kernel(a_ref, w_ref, r_ref, o_ref):
    k = pl.program_id(2)

    @pl.when(k == 0)
    def _():
        o_ref[...] = r_ref[...]

    o_ref[...] += jnp.dot(a_ref[...], w_ref[...].astype(BF16), preferred_element_type=F32)


def mm_ktiled(a, w, layer, residual, *, tm, tn, tk, name="mm_ktiled"):
    M, K = a.shape
    N = w.shape[2]
    return pl.pallas_call(
        _mm_kt_kernel,
        grid=(M // tm, N // tn, K // tk),
        in_specs=[pl.BlockSpec((tm, tk), lambda i, j, k: (i, k)),
                  pl.BlockSpec((None, tk, tn), lambda i, j, k: (layer, k, j)),
                  pl.BlockSpec((tm, tn), lambda i, j, k: (i, j))],
        out_specs=pl.BlockSpec((tm, tn), lambda i, j, k: (i, j)),
        out_shape=jax.ShapeDtypeStruct((M, N), F32),
        compiler_params=_params("parallel", "parallel", "arbitrary"),
        name=name,
    )(a, w, residual)


def _silu(x):
    return x * jax.nn.sigmoid(x)


def _ffn_in_prompt_kernel(a_ref, wg_ref, wu_ref, cw_ref, cb_ref, act_ref, st_ref,
                          wgb_ref, wub_ref, carry_ref, *, tiles_per_seq):
    i = pl.program_id(1)

    @pl.when(i == 0)
    def _():
        wgb_ref[...] = wg_ref[...].astype(BF16)
        wub_ref[...] = wu_ref[...].astype(BF16)

    @pl.when(i % tiles_per_seq == 0)
    def _():
        carry_ref[...] = jnp.zeros_like(carry_ref)

    a = a_ref[...]
    g = jnp.dot(a, wgb_ref[...], preferred_element_type=F32)
    u = jnp.dot(a, wub_ref[...], preferred_element_type=F32)
    tm = g.shape[0]
    prev = carry_ref[...]
    row = lax.broadcasted_iota(jnp.int32, g.shape, 0)
    g1 = jnp.where(row == 0, prev[7:8], pltpu.roll(g, 1, axis=0))
    g2 = jnp.where(row == 0, prev[6:7],
                   jnp.where(row == 1, prev[7:8], pltpu.roll(g, 2, axis=0)))
    cw = cw_ref[...]
    c = cb_ref[...] + cw[0:1] * g2 + cw[1:2] * g1 + cw[2:3] * g
    act_ref[...] = (_silu(c) * u).astype(act_ref.dtype)
    carry_ref[...] = g[tm - 8:tm]
    st_ref[...] = g[tm - (CONV_W - 1):tm]


def ffn_in_prompt(a, w_in, conv_w, conv_b, layer, *, seq, tm, tw):
    M, K = a.shape
    d_ff = w_in.shape[2] // 2
    nb = M // seq
    tps = seq // tm
    ub0 = d_ff // tw
    kern = functools.partial(_ffn_in_prompt_kernel, tiles_per_seq=tps)
    return pl.pallas_call(
        kern,
        grid=(d_ff // tw, M // tm),
        in_specs=[pl.BlockSpec((tm, K), lambda j, i: (i, 0)),
                  pl.BlockSpec((None, K, tw), lambda j, i: (layer, 0, j)),
                  pl.BlockSpec((None, K, tw), lambda j, i: (layer, 0, ub0 + j)),
                  pl.BlockSpec((None, CONV_W, tw), lambda j, i: (layer, 0, j)),
                  pl.BlockSpec((None, 1, tw), lambda j, i: (layer, 0, j))],
        out_specs=[pl.BlockSpec((tm, tw), lambda j, i: (i, j)),
                   pl.BlockSpec((None, CONV_W - 1, tw), lambda j, i: (i // tps, 0, j))],
        out_shape=[jax.ShapeDtypeStruct((M, d_ff), BF16),
                   jax.ShapeDtypeStruct((nb, CONV_W - 1, d_ff), F32)],
        scratch_shapes=[pltpu.VMEM((K, tw), BF16), pltpu.VMEM((K, tw), BF16),
                        pltpu.VMEM((8, tw), F32)],
        compiler_params=_params("parallel", "arbitrary"),
        name="ffn_in_prompt",
    )(a, w_in, w_in, conv_w, conv_b)


def _ffn_in_sample_kernel(a_ref, wg_ref, wu_ref, cw_ref, cb_ref, s0_ref, s1_ref, act_ref, g_ref):
    a = a_ref[...]
    g = jnp.dot(a, wg_ref[...].astype(BF16), preferred_element_type=F32)
    u = jnp.dot(a, wu_ref[...].astype(BF16), preferred_element_type=F32)
    cw = cw_ref[...]
    c = cb_ref[...] + cw[0:1] * s0_ref[...] + cw[1:2] * s1_ref[...] + cw[2:3] * g
    act_ref[...] = (_silu(c) * u).astype(act_ref.dtype)
    g_ref[...] = g


def ffn_in_sample(a, w_in, conv_w, conv_b, st0, st1, layer, *, tw):
    R, K = a.shape
    d_ff = w_in.shape[2] // 2
    ub0 = d_ff // tw
    return pl.pallas_call(
        _ffn_in_sample_kernel,
        grid=(d_ff // tw,),
        in_specs=[pl.BlockSpec((R, K), lambda j: (0, 0)),
                  pl.BlockSpec((None, K, tw), lambda j: (layer, 0, j)),
                  pl.BlockSpec((None, K, tw), lambda j: (layer, 0, ub0 + j)),
                  pl.BlockSpec((None, CONV_W, tw), lambda j: (layer, 0, j)),
                  pl.BlockSpec((None, 1, tw), lambda j: (layer, 0, j)),
                  pl.BlockSpec((R, tw), lambda j: (0, j)),
                  pl.BlockSpec((R, tw), lambda j: (0, j))],
        out_specs=[pl.BlockSpec((R, tw), lambda j: (0, j)),
                   pl.BlockSpec((R, tw), lambda j: (0, j))],
        out_shape=[jax.ShapeDtypeStruct((R, d_ff), BF16),
                   jax.ShapeDtypeStruct((R, d_ff), F32)],
        compiler_params=_params("parallel"),
        name="ffn_in_sample",
    )(a, w_in, w_in, conv_w, conv_b, st0, st1)


def _gla_tables(C):
    n_levels = int(math.log2(C))
    assert 1 << n_levels == C
    i = np.arange(C)[:, None]
    t = np.arange(C)[None, :]
    mats = [t <= i, t > i]
    level = np.full((C, C), -1, np.int32)
    level[np.arange(C), np.arange(C)] = n_levels
    for lv in range(n_levels):
        s = 1 << lv
        mid = (i // (2 * s)) * (2 * s) + s
        upper = i >= mid
        mats.append((upper & (t >= mid) & (t <= i)) | (~upper & (t > i) & (t < mid)))
        same = (i // (2 * s)) == (t // (2 * s))
        own = same & upper & ~((t % (2 * s)) >= s)
        level[own] = lv
    return np.concatenate(mats, axis=0).astype(np.float32), level, n_levels


def _nt(a, b):
    return lax.dot_general(a, b, (((1,), (1,)), ((), ())), preferred_element_type=F32)


def _tn(a, b):
    return lax.dot_general(a, b, (((0,), (0,)), ((), ())), preferred_element_type=F32)


def _gla_kernel(q_ref, k_ref, v_ref, r_ref, gl_ref, w2_ref, bg_ref, ng_ref, s0_ref, ms_ref, lvl_ref,
                o_ref, sout_ref, st_ref, *, C, n_levels, seq_len, dk):
    c = pl.program_id(2)
    nc = pl.num_programs(2)

    @pl.when(c == 0)
    def _():
        st_ref[...] = s0_ref[...].T

    z = jnp.dot(gl_ref[...].astype(BF16), w2_ref[...].astype(BF16), preferred_element_type=F32) + bg_ref[...]
    g = -(jnp.maximum(-z, 0.0) + jnp.log1p(jnp.exp(-jnp.abs(z)))) / GLA_GATE_TAU
    if seq_len % C:
        row = c * C + lax.broadcasted_iota(jnp.int32, g.shape, 0)
        g = jnp.where(row < seq_len, g, 0.0)
    g_hi = g.astype(BF16)
    g_lo = (g - g_hi.astype(F32)).astype(BF16)
    ms = ms_ref[...]
    ex = jnp.dot(ms, g_hi, preferred_element_type=F32) + jnp.dot(ms, g_lo, preferred_element_type=F32)
    e = jnp.exp(ex)

    q = q_ref[...] * (dk ** -0.5)
    k = k_ref[...]
    vb = v_ref[...].astype(BF16)
    e_b = e[0:C]
    e_end = e[C:2 * C]
    st = st_ref[...]
    o = _nt((q * e_b).astype(BF16), st.astype(BF16))
    lvl = lvl_ref[...]
    a = jnp.where(lvl == n_levels, _nt(q.astype(BF16), k.astype(BF16)), 0.0)
    for lv in range(n_levels):
        e_l = e[(2 + lv) * C:(3 + lv) * C]
        a = jnp.where(lvl == lv, _nt((q * e_l).astype(BF16), (k * e_l).astype(BF16)), a)
    o = o + jnp.dot(a.astype(BF16), vb, preferred_element_type=F32)
    decay_end = e_b[C - 1:C]
    st_ref[...] = st * decay_end + _tn(vb, (k * e_end).astype(BF16))

    y = o * lax.rsqrt(jnp.mean(o * o, axis=-1, keepdims=True) + EPS) * ng_ref[...]
    o_ref[...] = (y * _silu(r_ref[...])).astype(o_ref.dtype)

    @pl.when(c == nc - 1)
    def _():
        sout_ref[...] = st_ref[...].T


def gla_core(h, gl, w_gate2, b_gate, norm_gain, s0, layer, s0_layer, *, seq_len, C, n_heads, dk, dv):
    B, Lp, _ = h.shape
    ms_np, lvl_np, n_levels = _gla_tables(C)
    ms = jnp.asarray(ms_np, BF16)
    lvl = jnp.asarray(lvl_np)
    kd_blocks = n_heads
    v0 = 2 * n_heads * dk // dv
    r0 = v0 + n_heads
    kern = functools.partial(_gla_kernel, C=C, n_levels=n_levels, seq_len=seq_len, dk=dk)
    R = ms_np.shape[0]
    return pl.pallas_call(
        kern,
        grid=(B, n_heads, Lp // C),
        in_specs=[pl.BlockSpec((None, C, dk), lambda b, hd, c: (b, c, hd)),
                  pl.BlockSpec((None, C, dk), lambda b, hd, c: (b, c, kd_blocks + hd)),
                  pl.BlockSpec((None, C, dv), lambda b, hd, c: (b, c, v0 + hd)),
                  pl.BlockSpec((None, C, dv), lambda b, hd, c: (b, c, r0 + hd)),
                  pl.BlockSpec((None, C, LANES), lambda b, hd, c: (b, c, 0)),
                  pl.BlockSpec((None, LANES, dk), lambda b, hd, c: (layer, 0, hd)),
                  pl.BlockSpec((None, 1, dk), lambda b, hd, c: (layer, 0, hd)),
                  pl.BlockSpec((None, 1, dv), lambda b, hd, c: (layer, 0, 0)),
                  pl.BlockSpec((None, None, None, dk, dv), lambda b, hd, c: (s0_layer, b, hd, 0, 0)),
                  pl.BlockSpec((R, C), lambda b, hd, c: (0, 0)),
                  pl.BlockSpec((C, C), lambda b, hd, c: (0, 0))],
        out_specs=[pl.BlockSpec((None, C, dv), lambda b, hd, c: (b, c, hd)),
                   pl.BlockSpec((None, None, dk, dv), lambda b, hd, c: (b, hd, 0, 0))],
        out_shape=[jax.ShapeDtypeStruct((B, Lp, n_heads * dv), BF16),
                   jax.ShapeDtypeStruct((B, n_heads, dk, dv), F32)],
        scratch_shapes=[pltpu.VMEM((dv, dk), F32)],
        compiler_params=_params("parallel", "parallel", "arbitrary"),
        name="gla_core",
    )(h, h, h, h, gl, w_gate2, b_gate, norm_gain, s0, ms, lvl)


def _rope_tables(pos):
    half = ROT_DIM // 2
    inv = jnp.power(ROPE_THETA, -jnp.arange(half, dtype=F32) * (2.0 / ROT_DIM))
    ang = pos.astype(F32)[:, None] * inv[None, :]
    cos, sin = jnp.cos(ang), jnp.sin(ang)
    n = pos.shape[0]
    cos_t = jnp.concatenate([cos, cos, jnp.ones((n, HEAD_DIM - ROT_DIM), F32)], axis=1)
    sin_t = jnp.concatenate([-sin, sin, jnp.zeros((n, HEAD_DIM - ROT_DIM), F32)], axis=1)
    return cos_t, sin_t


def _qkv_epilogue(acc, o_ref, gain_ref, cos_ref, sin_ref, *, tn, section, head_major):
    j = pl.program_id(0)
    kind = (j * tn // section) % 3

    def put(hh, val):
        if head_major:
            o_ref[hh] = val
        else:
            o_ref[:, hh * HEAD_DIM:(hh + 1) * HEAD_DIM] = val

    @pl.when(kind == 2)
    def _():
        for hh in range(tn // HEAD_DIM):
            put(hh, acc[:, hh * HEAD_DIM:(hh + 1) * HEAD_DIM])

    @pl.when(kind != 2)
    def _():
        half = ROT_DIM // 2
        cos_t = cos_ref[...]
        sin_t = sin_ref[...]
        gain = gain_ref[...]
        lane = lax.broadcasted_iota(jnp.int32, cos_t.shape, 1)
        for hh in range(tn // HEAD_DIM):
            x = acc[:, hh * HEAD_DIM:(hh + 1) * HEAD_DIM]
            y = x * lax.rsqrt(jnp.mean(x * x, axis=-1, keepdims=True) + EPS) * gain
            swapped = jnp.where(lane < half, pltpu.roll(y, HEAD_DIM - half, axis=1), pltpu.roll(y, half, axis=1))
            put(hh, y * cos_t + swapped * sin_t)


def dsa_qkv(a, w_qkv, gains, cos_t, sin_t, layer, *, tm, tn, section, seq=None):
    M = a.shape[0]
    n_tab = cos_t.shape[0] // tm
    n_sec = w_qkv.shape[2] // section
    extra_specs = [pl.BlockSpec((None, 1, HEAD_DIM), lambda j, i: (layer * n_sec + j * tn // section, 0, 0)),
                   pl.BlockSpec((tm, HEAD_DIM), lambda j, i: (i % n_tab, 0)),
                   pl.BlockSpec((tm, HEAD_DIM), lambda j, i: (i % n_tab, 0))]
    epi = functools.partial(_qkv_epilogue, tn=tn, section=section, head_major=seq is not None)
    out_spec = out_shape = None
    if seq is not None:
        tps = seq // tm
        hpb = tn // HEAD_DIM
        out_spec = pl.BlockSpec((None, hpb, tm, HEAD_DIM), lambda j, i: (i // tps, j, i % tps, 0))
        out_shape = jax.ShapeDtypeStruct((M // seq, w_qkv.shape[2] // HEAD_DIM, seq, HEAD_DIM), F32)
    return mm_wres(a, w_qkv, layer, tm=tm, tn=tn, epilogue=epi, extra=(gains, cos_t, sin_t),
                   extra_specs=extra_specs, out_spec=out_spec, out_shape=out_shape, name="dsa_qkv")


def _rows(start, size, stride):
    if stride == 1:
        return (pl.ds(start, size), slice(None))
    return (pl.ds(start, size, stride=stride), slice(None))


def _prompt_attn_kernel(*refs, seq, merge_rows):
    qkv_refs = refs[:3 * N_GROUPS]
    out_ref = refs[3 * N_GROUPS]
    o_scr = refs[3 * N_GROUPS + 1:3 * N_GROUPS + 1 + N_GROUPS]
    l_scr = refs[3 * N_GROUPS + 1 + N_GROUPS:]
    scale = HEAD_DIM ** -0.5
    for g, (window, dil) in enumerate(DSA_GROUPS):
        q_ref, k_ref, v_ref = qkv_refs[3 * g:3 * g + 3]
        blk = window // dil
        span = blk * dil
        nb = seq // span
        qi = lax.broadcasted_iota(jnp.int32, (blk, 2 * blk), 0)
        ki = lax.broadcasted_iota(jnp.int32, (blk, 2 * blk), 1)
        both_ok = (ki >= qi) & (ki <= qi + blk)
        first_ok = (lax.broadcasted_iota(jnp.int32, (blk, blk), 1)
                    <= lax.broadcasted_iota(jnp.int32, (blk, blk), 0))
        for r in range(dil):
            for n in range(nb):
                start = r + n * span
                q = q_ref[_rows(start, blk, dil)].astype(BF16)
                if n == 0:
                    kk = k_ref[_rows(start, blk, dil)].astype(BF16)
                    vv = v_ref[_rows(start, blk, dil)].astype(BF16)
                    ok = first_ok
                else:
                    kk = k_ref[_rows(start - span, 2 * blk, dil)].astype(BF16)
                    vv = v_ref[_rows(start - span, 2 * blk, dil)].astype(BF16)
                    ok = both_ok
                s = jnp.where(ok, _nt(q, kk) * scale, -jnp.inf)
                mx = jnp.max(s, axis=-1, keepdims=True)
                p = jnp.exp(s - mx)
                den = jnp.sum(p, axis=-1, keepdims=True)
                o = jnp.dot(p.astype(BF16), vv, preferred_element_type=F32) / den
                o_scr[g][_rows(start, blk, dil)] = o
                l_scr[g][_rows(start, blk, dil)] = jnp.broadcast_to(mx + jnp.log(den), (blk, HEAD_DIM))
    for c in range(seq // merge_rows):
        sl = slice(c * merge_rows, (c + 1) * merge_rows)
        ls = [l_scr[g][sl, :] for g in range(N_GROUPS)]
        mx = functools.reduce(jnp.maximum, ls)
        es = [jnp.exp(l - mx) for l in ls]
        den = functools.reduce(lambda a, b: a + b, es)
        acc = es[0] * o_scr[0][sl, :]
        for g in range(1, N_GROUPS):
            acc = acc + es[g] * o_scr[g][sl, :]
        out_ref[sl, :] = (acc / den).astype(out_ref.dtype)


def prompt_attention(qkvh, *, n_heads):
    B, _, S, _ = qkvh.shape
    for window, dil in DSA_GROUPS:
        assert S % window == 0
    in_specs = []
    for g in range(N_GROUPS):
        for t in range(3):
            in_specs.append(pl.BlockSpec((None, None, S, HEAD_DIM),
                                         lambda b, h, g=g, t=t: (b, (g * 3 + t) * n_heads + h, 0, 0)))
    kern = functools.partial(_prompt_attn_kernel, seq=S, merge_rows=min(256, S))
    return pl.pallas_call(
        kern,
        grid=(B, n_heads),
        in_specs=in_specs,
        out_specs=pl.BlockSpec((None, S, HEAD_DIM), lambda b, h: (b, 0, h)),
        out_shape=jax.ShapeDtypeStruct((B, S, n_heads * HEAD_DIM), BF16),
        scratch_shapes=[pltpu.VMEM((S, HEAD_DIM), F32)] * (2 * N_GROUPS),
        compiler_params=_params("parallel", "parallel"),
        name="prompt_attention",
    )(*([qkvh] * (3 * N_GROUPS)))


def _kv_export_kernel(*refs, n_layers, n_heads, ts):
    out_ref = refs[2 * n_layers]
    layer = pl.program_id(0)
    for l in range(n_layers):
        k_ref, v_ref = refs[2 * l], refs[2 * l + 1]

        @pl.when(layer == l)
        def _():
            for h in range(n_heads):
                out_ref[pl.ds(h, ts, stride=2 * n_heads), :] = k_ref[h]
                out_ref[pl.ds(n_heads + h, ts, stride=2 * n_heads), :] = v_ref[h]


def kv_export(qkvh_layers, group, *, keep, n_heads, ts):
    n_layers = len(qkvh_layers)
    B, _, S, _ = qkvh_layers[0].shape
    nt = keep // ts
    t0 = (S - keep) // ts
    in_specs, args = [], []
    for l, arr in enumerate(qkvh_layers):
        for t in (1, 2):
            def idx(ll, b, i, l=l, t=t):
                bb = jnp.where(ll == l, b, jnp.where(ll < l, 0, B - 1))
                ii = jnp.where(ll == l, i, jnp.where(ll < l, 0, nt - 1))
                return (bb, group * 3 + t, t0 + ii, 0)
            in_specs.append(pl.BlockSpec((None, n_heads, ts, HEAD_DIM), idx))
            args.append(arr)
    rows = 2 * n_heads
    out = pl.pallas_call(
        functools.partial(_kv_export_kernel, n_layers=n_layers, n_heads=n_heads, ts=ts),
        grid=(n_layers, B, nt),
        in_specs=in_specs,
        out_specs=pl.BlockSpec((None, None, ts * rows, HEAD_DIM), lambda ll, b, i: (ll, b, i, 0)),
        out_shape=jax.ShapeDtypeStruct((n_layers, B, keep * rows, HEAD_DIM), F32),
        compiler_params=_params("arbitrary", "arbitrary", "arbitrary"),
        name=f"kv_export_g{group}",
    )(*args)
    return out.reshape(n_layers, B, keep, 2, n_heads, HEAD_DIM)


def _sample_attn_kernel(qkv_ref, c0_ref, c1_ref, c2_ref, out_ref, *, n_heads):
    scale = HEAD_DIM ** -0.5
    os, ls = [], []
    for g, c_ref in enumerate((c0_ref, c1_ref, c2_ref)):
        q, k_new, v_new = qkv_ref[3 * g], qkv_ref[3 * g + 1], qkv_ref[3 * g + 2]
        kc = c_ref[:, 0:n_heads, :]
        vc = c_ref[:, n_heads:2 * n_heads, :]
        s_c = jnp.sum(kc * q[None], axis=-1, keepdims=True) * scale
        s_n = jnp.sum(k_new * q, axis=-1, keepdims=True) * scale
        mx = jnp.maximum(jnp.max(s_c, axis=0), s_n)
        p_c = jnp.exp(s_c - mx[None])
        p_n = jnp.exp(s_n - mx)
        den = jnp.sum(p_c, axis=0) + p_n
        os.append((jnp.sum(p_c * vc, axis=0) + p_n * v_new) / den)
        ls.append(mx + jnp.log(den))
    mx = functools.reduce(jnp.maximum, ls)
    es = [jnp.exp(l - mx) for l in ls]
    den = functools.reduce(lambda a, b: a + b, es)
    acc = es[0] * os[0]
    for g in range(1, N_GROUPS):
        acc = acc + es[g] * os[g]
    out_ref[...] = (acc / den).astype(out_ref.dtype)


def sample_attention(qkv, caches, layer, *, n_heads):
    B = qkv.shape[0]
    rows = 2 * n_heads
    views, specs = [], []
    for (window, dil), c in zip(DSA_GROUPS, caches):
        n_buf = c.shape[2]
        assert n_buf == window and n_buf % dil == 0
        views.append(c.reshape(c.shape[0], B, n_buf // dil, dil * rows, HEAD_DIM))
        specs.append(pl.BlockSpec((None, None, n_buf // dil, rows, HEAD_DIM), lambda b: (layer, b, 0, 0, 0)))
    return pl.pallas_call(
        functools.partial(_sample_attn_kernel, n_heads=n_heads),
        grid=(B,),
        in_specs=[pl.BlockSpec((None, 3 * N_GROUPS, n_heads, HEAD_DIM), lambda b: (b, 0, 0, 0))] + specs,
        out_specs=pl.BlockSpec((None, n_heads, HEAD_DIM), lambda b: (b, 0, 0)),
        out_shape=jax.ShapeDtypeStruct((B, n_heads, HEAD_DIM), F32),
        compiler_params=_params("parallel"),
        name="sample_attention",
    )(qkv, *views)


def _cache_shift_kernel(*refs, n_groups, n_layers, n_batch):
    caches = refs[:n_groups]
    news = refs[n_groups:2 * n_groups]
    outs = refs[2 * n_groups:3 * n_groups]
    sem = refs[3 * n_groups]
    copies = []
    for g in range(n_groups):
        rows = news[g].shape[2]
        total = caches[g].shape[2]
        for l in range(n_layers):
            for b in range(n_batch):
                copies.append(pltpu.make_async_copy(caches[g].at[l, b, pl.ds(rows, total - rows)],
                                                    outs[g].at[l, b, pl.ds(0, total - rows)],
                                                    sem.at[len(copies)]))
                copies.append(pltpu.make_async_copy(news[g].at[l, b],
                                                    outs[g].at[l, b, pl.ds(total - rows, rows)],
                                                    sem.at[len(copies)]))
    for cp in copies:
        cp.start()
    for cp in copies:
        cp.wait()


def cache_shift(caches, news):
    n_groups = len(caches)
    NL, B = caches[0].shape[:2]
    flat = [c.reshape(NL, B, -1, HEAD_DIM) for c in caches]
    flat_new = [n.reshape(NL, B, -1, HEAD_DIM) for n in news]
    any_spec = pl.BlockSpec(memory_space=pl.ANY)
    outs = pl.pallas_call(
        functools.partial(_cache_shift_kernel, n_groups=n_groups, n_layers=NL, n_batch=B),
        in_specs=[any_spec] * (2 * n_groups),
        out_specs=[any_spec] * n_groups,
        out_shape=[jax.ShapeDtypeStruct(f.shape, f.dtype) for f in flat],
        scratch_shapes=[pltpu.SemaphoreType.DMA((2 * n_groups * NL * B,))],
        name="cache_shift",
    )(*flat, *flat_new)
    return [o.reshape(c.shape) for o, c in zip(outs, caches)]


def kernel(x_prompt, x_sample, cache_kv_w128, cache_kv_w512, cache_kv_w2048, state_gla, state_ffn_conv,
           norm_mix, norm_ffn, dsa_w_qkv, dsa_q_gain, dsa_k_gain, dsa_w_o,
           gla_w_in, gla_w_gate2, gla_b_gate, gla_norm_gain, gla_w_o,
           ffn_w_in, ffn_conv_w, ffn_conv_b, ffn_w_out):
    B, S, D = x_prompt.shape
    BS = x_sample.shape[0]
    assert x_sample.shape[1] == 1
    depth = norm_mix.shape[0]
    caches = (cache_kv_w128, cache_kv_w512, cache_kv_w2048)
    n_gla, gla_heads, gla_dk, gla_dv = state_gla.shape[0], state_gla.shape[2], state_gla.shape[3], state_gla.shape[4]
    gla_kd, gla_vd = gla_heads * gla_dk, gla_heads * gla_dv
    dsa_heads = dsa_w_o.shape[1] // HEAD_DIM
    W = dsa_heads * HEAD_DIM
    d_ff = ffn_w_in.shape[2] // 2
    M = B * S
    R = SAMPLE_ROWS

    tm = min(1024, S)
    tn = min(512, W)
    tw = 256
    tn_out = min(2048, D)
    gla_chunk = min(128, S)

    xp = x_prompt.reshape(M, D)
    xs = jnp.pad(x_sample.reshape(BS, D), ((0, R - BS), (0, 0)))

    norm_mix3 = norm_mix.reshape(depth, 1, D)
    norm_ffn3 = norm_ffn.reshape(depth, 1, D)
    conv_b3 = ffn_conv_b.reshape(depth, 1, d_ff)
    n_main = 2 * gla_kd + 2 * gla_vd
    w_gl = jnp.pad(gla_w_in[:, :, n_main:], ((0, 0), (0, 0), (0, LANES - GLA_GATE_RANK)))
    w_gate2 = jnp.pad(gla_w_gate2, ((0, 0), (0, LANES - GLA_GATE_RANK), (0, 0)))
    b_gate3 = gla_b_gate.reshape(n_gla, 1, gla_kd)
    gla_ng3 = gla_norm_gain.reshape(n_gla, 1, gla_dv)
    ones = jnp.ones_like(dsa_q_gain)
    qk_gains = jnp.stack([dsa_q_gain, dsa_k_gain, ones], axis=2).reshape(-1, 1, HEAD_DIM)
    cos_p, sin_p = _rope_tables(jnp.arange(S))
    cos_s, sin_s = _rope_tables(jnp.full((R,), PAST_LEN, jnp.int32))

    qkvh_layers = []
    kv_new = [[] for _ in DSA_GROUPS]
    gla_p, gla_s, conv_p, conv_s = [], [], [], []
    zero_state = jnp.zeros((1, B, gla_heads, gla_dk, gla_dv), F32)

    for i in range(depth):
        j = i // 2
        hp = rms_bf16(xp, norm_mix3, i, 256)
        hs = rms_bf16(xs, norm_mix3, i, R)
        if i % 2 == 0:
            h = mm_wres(hp, gla_w_in, j, tm=tm, tn=tn, ncols=n_main, name="gla_in")
            gl = mm_wres(hp, w_gl, j, tm=tm, tn=LANES, name="gla_gate_in")
            o, sp = gla_core(h.reshape(B, S, n_main), gl.reshape(B, S, LANES), w_gate2, b_gate3, gla_ng3,
                             zero_state, j, 0, seq_len=S, C=gla_chunk, n_heads=gla_heads, dk=gla_dk, dv=gla_dv)
            xp = mm_wres(o.reshape(M, gla_vd), gla_w_o, j, tm=tm, tn=tn, residual=xp, name="gla_out")
            h_s = mm_wres(hs, gla_w_in, j, tm=R, tn=tn, ncols=n_main, name="gla_in_s")
            gl_s = mm_wres(hs, w_gl, j, tm=R, tn=LANES, name="gla_gate_in_s")
            h_s = jnp.pad(h_s[:BS].reshape(BS, 1, n_main), ((0, 0), (0, R - 1), (0, 0)))
            gl_s = jnp.pad(gl_s[:BS].reshape(BS, 1, LANES), ((0, 0), (0, R - 1), (0, 0)))
            o_s, ss = gla_core(h_s, gl_s, w_gate2, b_gate3, gla_ng3, state_gla, j, j,
                               seq_len=1, C=R, n_heads=gla_heads, dk=gla_dk, dv=gla_dv)
            o_s = jnp.pad(o_s[:, 0], ((0, R - BS), (0, 0)))
            xs = mm_wres(o_s, gla_w_o, j, tm=R, tn=tn, residual=xs, name="gla_out_s")
            gla_p.append(sp)
            gla_s.append(ss)
        else:
            qkvh = dsa_qkv(hp, dsa_w_qkv, qk_gains, cos_p, sin_p, j, tm=tm, tn=tn, section=W, seq=S)
            merged = prompt_attention(qkvh, n_heads=dsa_heads)
            xp = mm_wres(merged.reshape(M, W), dsa_w_o, j, tm=tm, tn=tn, residual=xp, name="dsa_out")
            qkvh_layers.append(qkvh)
            qkv_s = dsa_qkv(hs, dsa_w_qkv, qk_gains, cos_s, sin_s, j, tm=R, tn=tn, section=W)
            qkv_s = qkv_s[:BS].reshape(BS, N_GROUPS, 3, dsa_heads, HEAD_DIM)
            merged_s = sample_attention(qkv_s.reshape(BS, 3 * N_GROUPS, dsa_heads, HEAD_DIM), caches, j,
                                        n_heads=dsa_heads)
            merged_s = jnp.pad(merged_s.reshape(BS, W).astype(BF16), ((0, R - BS), (0, 0)))
            xs = mm_wres(merged_s, dsa_w_o, j, tm=R, tn=tn, residual=xs, name="dsa_out_s")
            for g in range(N_GROUPS):
                kv_new[g].append(qkv_s[:, g, 1:3])
        fp = rms_bf16(xp, norm_ffn3, i, 256)
        act, cp = ffn_in_prompt(fp, ffn_w_in, ffn_conv_w, conv_b3, i, seq=S, tm=tm, tw=tw)
        xp = mm_ktiled(act, ffn_w_out, i, xp, tm=tm, tn=tn_out, tk=tw, name="ffn_out")
        conv_p.append(cp)
        fs = rms_bf16(xs, norm_ffn3, i, R)
        st = jnp.pad(state_ffn_conv[i], ((0, R - BS), (0, 0), (0, 0)))
        act_s, g_s = ffn_in_sample(fs, ffn_w_in, ffn_conv_w, conv_b3, st[:, 0], st[:, 1], i, tw=tw)
        xs = mm_ktiled(act_s, ffn_w_out, i, xs, tm=R, tn=tn_out, tk=tw, name="ffn_out_s")
        conv_s.append(jnp.stack([state_ffn_conv[i][:, 1], g_s[:BS]], axis=1))

    kv128_p, kv512_p, kv2048_p = [
        kv_export(qkvh_layers, g, keep=min(window, S), n_heads=dsa_heads, ts=min(128, S))
        for g, (window, _) in enumerate(DSA_GROUPS)]
    kv128_s, kv512_s, kv2048_s = cache_shift(caches, [jnp.stack(n, axis=0) for n in kv_new])
    return (xp.reshape(B, S, D), xs[:BS].reshape(BS, 1, D),
            kv128_p, kv128_s, kv512_p, kv512_s, kv2048_p, kv2048_s,
            jnp.stack(gla_p, axis=0), jnp.stack(gla_s, axis=0),
            jnp.stack(conv_p, axis=0), jnp.stack(conv_s, axis=0))
```

```python
import functools
import math

import numpy as np
import jax
import jax.numpy as jnp
from jax import lax
from jax.experimental import pallas as pl
from jax.experimental.pallas import tpu as pltpu

F32 = jnp.float32
BF16 = jnp.bfloat16

EPS = 1e-6
HEAD_DIM = 128
ROT_DIM = HEAD_DIM // 4
ROPE_THETA = 500000.0
DSA_GROUPS = ((128, 1), (512, 4), (2048, 16))
N_GROUPS = len(DSA_GROUPS)
GLA_GATE_RANK = 16
GLA_GATE_TAU = 16.0
CONV_W = 3
PAST_LEN = 8192
LANES = 128
SAMPLE_ROWS = 16
VMEM_LIMIT_BYTES = 56 * 1024 * 1024


def _params(*sem):
    return pltpu.CompilerParams(dimension_semantics=sem, vmem_limit_bytes=VMEM_LIMIT_BYTES)


def _rms_kernel(x_ref, g_ref, o_ref):
    x = x_ref[...]
    y = x * lax.rsqrt(jnp.mean(x * x, axis=-1, keepdims=True) + EPS)
    o_ref[...] = (y * g_ref[...]).astype(o_ref.dtype)


def rms_bf16(x, gains, layer, tm):
    M, D = x.shape
    return pl.pallas_call(
        _rms_kernel,
        grid=(M // tm,),
        in_specs=[pl.BlockSpec((tm, D), lambda i: (i, 0)),
                  pl.BlockSpec((None, 1, D), lambda i: (layer, 0, 0))],
        out_specs=pl.BlockSpec((tm, D), lambda i: (i, 0)),
        out_shape=jax.ShapeDtypeStruct((M, D), BF16),
        compiler_params=_params("parallel"),
        name="rms_bf16",
    )(x, gains)


def _mm_wres_kernel(a_ref, w_ref, *rest, has_res, epilogue):
    n_extra = len(rest) - 2 - (1 if has_res else 0)
    extra = rest[:n_extra]
    rest = rest[n_extra:]
    if has_res:
        r_ref, o_ref, wb_ref = rest
    else:
        o_ref, wb_ref = rest
    i = pl.program_id(1)

    @pl.when(i == 0)
    def _():
        wb_ref[...] = w_ref[...].astype(BF16)

    def product(rows=slice(None)):
        acc = jnp.dot(a_ref[rows, :], wb_ref[...], preferred_element_type=F32)
        if has_res:
            acc = acc + r_ref[rows, :]
        return acc

    if epilogue is None:
        o_ref[...] = product().astype(o_ref.dtype)
    else:
        epilogue(product, o_ref, *extra)


def mm_wres(a, w, layer, *, tm, tn, col0=0, ncols=None, out_dtype=F32, residual=None,
            epilogue=None, extra=(), extra_specs=(), out_spec=None, out_shape=None, name="mm_wres"):
    M, K = a.shape
    ncols = w.shape[2] - col0 if ncols is None else ncols
    cb0 = col0 // tn
    if out_spec is None:
        out_spec = pl.BlockSpec((tm, tn), lambda j, i: (i, j))
        out_shape = jax.ShapeDtypeStruct((M, ncols), out_dtype)
    in_specs = [pl.BlockSpec((tm, K), lambda j, i: (i, 0)),
                pl.BlockSpec((None, K, tn), lambda j, i: (layer, 0, cb0 + j))]
    in_specs += list(extra_specs)
    args = [a, w, *extra]
    if residual is not None:
        in_specs.append(pl.BlockSpec((tm, tn), lambda j, i: (i, j)))
        args.append(residual)
    kern = functools.partial(_mm_wres_kernel, has_res=residual is not None, epilogue=epilogue)
    return pl.pallas_call(
        kern,
        grid=(ncols // tn, M // tm),
        in_specs=in_specs,
        out_specs=out_spec,
        out_shape=out_shape,
        scratch_shapes=[pltpu.VMEM((K, tn), BF16)],
        compiler_params=_params("parallel", "arbitrary"),
        name=name,
    )(*args)


def _mm_kt_kernel(a_ref, w_ref, r_ref, o_ref):
    k = pl.program_id(2)

    @pl.when(k == 0)
    def _():
        o_ref[...] = r_ref[...]

    o_ref[...] += jnp.dot(a_ref[...], w_ref[...].astype(BF16), preferred_element_type=F32)


def mm_ktiled(a, w, layer, residual, *, tm, tn, tk, name="mm_ktiled"):
    M, K = a.shape
    N = w.shape[2]
    return pl.pallas_call(
        _mm_kt_kernel,
        grid=(M // tm, N // tn, K // tk),
        in_specs=[pl.BlockSpec((tm, tk), lambda i, j, k: (i, k)),
                  pl.BlockSpec((None, tk, tn), lambda i, j, k: (layer, k, j)),
                  pl.BlockSpec((tm, tn), lambda i, j, k: (i, j))],
        out_specs=pl.BlockSpec((tm, tn), lambda i, j, k: (i, j)),
        out_shape=jax.ShapeDtypeStruct((M, N), F32),
        compiler_params=_params("parallel", "parallel", "arbitrary"),
        name=name,
    )(a, w, residual)


def _silu(x):
    return x * jax.nn.sigmoid(x)


def _ffn_in_prompt_kernel(a_ref, wg_ref, wu_ref, cw_ref, cb_ref, act_ref, st_ref,
                          wgb_ref, wub_ref, carry_ref, *, tiles_per_seq, row_chunk):
    i = pl.program_id(1)

    @pl.when(i == 0)
    def _():
        wgb_ref[...] = wg_ref[...].astype(BF16)
        wub_ref[...] = wu_ref[...].astype(BF16)

    @pl.when(i % tiles_per_seq == 0)
    def _():
        carry_ref[...] = jnp.zeros_like(carry_ref)

    tm = a_ref.shape[0]
    prev = carry_ref[...]
    row = lax.broadcasted_iota(jnp.int32, (row_chunk, wgb_ref.shape[1]), 0)
    cw = cw_ref[...]
    cb = cb_ref[...]
    for r in range(0, tm, row_chunk):
        a = a_ref[r:r + row_chunk, :]
        g = jnp.dot(a, wgb_ref[...], preferred_element_type=F32)
        u = jnp.dot(a, wub_ref[...], preferred_element_type=F32)
        g1 = jnp.where(row == 0, prev[7:8], pltpu.roll(g, 1, axis=0))
        g2 = jnp.where(row == 0, prev[6:7],
                       jnp.where(row == 1, prev[7:8], pltpu.roll(g, 2, axis=0)))
        c = cb + cw[0:1] * g2 + cw[1:2] * g1 + cw[2:3] * g
        act_ref[r:r + row_chunk, :] = (_silu(c) * u).astype(act_ref.dtype)
        prev = g[row_chunk - 8:row_chunk]
    carry_ref[...] = prev
    st_ref[...] = prev[8 - (CONV_W - 1):8]


def ffn_in_prompt(a, w_in, conv_w, conv_b, layer, *, seq, tm, tw):
    M, K = a.shape
    d_ff = w_in.shape[2] // 2
    nb = M // seq
    tps = seq // tm
    ub0 = d_ff // tw
    kern = functools.partial(_ffn_in_prompt_kernel, tiles_per_seq=tps, row_chunk=min(256, tm))
    return pl.pallas_call(
        kern,
        grid=(d_ff // tw, M // tm),
        in_specs=[pl.BlockSpec((tm, K), lambda j, i: (i, 0)),
                  pl.BlockSpec((None, K, tw), lambda j, i: (layer, 0, j)),
                  pl.BlockSpec((None, K, tw), lambda j, i: (layer, 0, ub0 + j)),
                  pl.BlockSpec((None, CONV_W, tw), lambda j, i: (layer, 0, j)),
                  pl.BlockSpec((None, 1, tw), lambda j, i: (layer, 0, j))],
        out_specs=[pl.BlockSpec((tm, tw), lambda j, i: (i, j)),
                   pl.BlockSpec((None, CONV_W - 1, tw), lambda j, i: (i // tps, 0, j))],
        out_shape=[jax.ShapeDtypeStruct((M, d_ff), BF16),
                   jax.ShapeDtypeStruct((nb, CONV_W - 1, d_ff), F32)],
        scratch_shapes=[pltpu.VMEM((K, tw), BF16), pltpu.VMEM((K, tw), BF16),
                        pltpu.VMEM((8, tw), F32)],
        compiler_params=_params("parallel", "arbitrary"),
        name="ffn_in_prompt",
    )(a, w_in, w_in, conv_w, conv_b)


def _ffn_in_sample_kernel(a_ref, wg_ref, wu_ref, cw_ref, cb_ref, s0_ref, s1_ref, act_ref, g_ref):
    a = a_ref[...]
    g = jnp.dot(a, wg_ref[...].astype(BF16), preferred_element_type=F32)
    u = jnp.dot(a, wu_ref[...].astype(BF16), preferred_element_type=F32)
    cw = cw_ref[...]
    c = cb_ref[...] + cw[0:1] * s0_ref[...] + cw[1:2] * s1_ref[...] + cw[2:3] * g
    act_ref[...] = (_silu(c) * u).astype(act_ref.dtype)
    g_ref[...] = g


def ffn_in_sample(a, w_in, conv_w, conv_b, st0, st1, layer, *, tw):
    R, K = a.shape
    d_ff = w_in.shape[2] // 2
    ub0 = d_ff // tw
    return pl.pallas_call(
        _ffn_in_sample_kernel,
        grid=(d_ff // tw,),
        in_specs=[pl.BlockSpec((R, K), lambda j: (0, 0)),
                  pl.BlockSpec((None, K, tw), lambda j: (layer, 0, j)),
                  pl.BlockSpec((None, K, tw), lambda j: (layer, 0, ub0 + j)),
                  pl.BlockSpec((None, CONV_W, tw), lambda j: (layer, 0, j)),
                  pl.BlockSpec((None, 1, tw), lambda j: (layer, 0, j)),
                  pl.BlockSpec((R, tw), lambda j: (0, j)),
                  pl.BlockSpec((R, tw), lambda j: (0, j))],
        out_specs=[pl.BlockSpec((R, tw), lambda j: (0, j)),
                   pl.BlockSpec((R, tw), lambda j: (0, j))],
        out_shape=[jax.ShapeDtypeStruct((R, d_ff), BF16),
                   jax.ShapeDtypeStruct((R, d_ff), F32)],
        compiler_params=_params("parallel"),
        name="ffn_in_sample",
    )(a, w_in, w_in, conv_w, conv_b, st0, st1)


def _gla_tables(C):
    n_levels = int(math.log2(C))
    assert 1 << n_levels == C
    i = np.arange(C)[:, None]
    t = np.arange(C)[None, :]
    mats = [t <= i, t > i]
    level = np.full((C, C), -1, np.int32)
    level[np.arange(C), np.arange(C)] = n_levels
    for lv in range(n_levels):
        s = 1 << lv
        mid = (i // (2 * s)) * (2 * s) + s
        upper = i >= mid
        mats.append((upper & (t >= mid) & (t <= i)) | (~upper & (t > i) & (t < mid)))
        same = (i // (2 * s)) == (t // (2 * s))
        own = same & upper & ~((t % (2 * s)) >= s)
        level[own] = lv
    return np.concatenate(mats, axis=0).astype(np.float32), level, n_levels


def _nt(a, b):
    return lax.dot_general(a, b, (((1,), (1,)), ((), ())), preferred_element_type=F32)


def _tn(a, b):
    return lax.dot_general(a, b, (((0,), (0,)), ((), ())), preferred_element_type=F32)


def _gla_kernel(q_ref, k_ref, v_ref, r_ref, gl_ref, w2_ref, bg_ref, ng_ref, s0_ref, ms_ref, lvl_ref,
                o_ref, sout_ref, st_ref, *, C, n_levels, seq_len, dk, dv, heads):
    c = pl.program_id(2)
    nc = pl.num_programs(2)

    @pl.when(c == 0)
    def _():
        for hh in range(heads):
            st_ref[hh] = s0_ref[hh].T

    glb = gl_ref[...].astype(BF16)
    ms = ms_ref[...]
    lvl = lvl_ref[...]
    for hh in range(heads):
        ks = slice(hh * dk, (hh + 1) * dk)
        vs = slice(hh * dv, (hh + 1) * dv)
        z = jnp.dot(glb, w2_ref[:, ks].astype(BF16), preferred_element_type=F32) + bg_ref[:, ks]
        g = -(jnp.maximum(-z, 0.0) + jnp.log1p(jnp.exp(-jnp.abs(z)))) / GLA_GATE_TAU
        if seq_len % C:
            row = c * C + lax.broadcasted_iota(jnp.int32, g.shape, 0)
            g = jnp.where(row < seq_len, g, 0.0)
        g_hi = g.astype(BF16)
        g_lo = (g - g_hi.astype(F32)).astype(BF16)
        ex = jnp.dot(ms, g_hi, preferred_element_type=F32) + jnp.dot(ms, g_lo, preferred_element_type=F32)
        e = jnp.exp(ex)

        q = q_ref[:, ks] * (dk ** -0.5)
        k = k_ref[:, ks]
        vb = v_ref[:, vs].astype(BF16)
        e_b = e[0:C]
        e_end = e[C:2 * C]
        st = st_ref[hh]
        o = _nt((q * e_b).astype(BF16), st.astype(BF16))
        a = jnp.where(lvl == n_levels, _nt(q.astype(BF16), k.astype(BF16)), 0.0)
        for lv in range(n_levels):
            e_l = e[(2 + lv) * C:(3 + lv) * C]
            a = jnp.where(lvl == lv, _nt((q * e_l).astype(BF16), (k * e_l).astype(BF16)), a)
        o = o + jnp.dot(a.astype(BF16), vb, preferred_element_type=F32)
        decay_end = e_b[C - 1:C]
        st_ref[hh] = st * decay_end + _tn(vb, (k * e_end).astype(BF16))

        y = o * lax.rsqrt(jnp.mean(o * o, axis=-1, keepdims=True) + EPS) * ng_ref[...]
        o_ref[:, vs] = (y * _silu(r_ref[:, vs])).astype(o_ref.dtype)

    @pl.when(c == nc - 1)
    def _():
        for hh in range(heads):
            sout_ref[hh] = st_ref[hh].T


def gla_core(h, gl, w_gate2, b_gate, norm_gain, s0, layer, s0_layer, *, seq_len, C, n_heads, dk, dv,
             heads_per_step):
    B, Lp, _ = h.shape
    ms_np, lvl_np, n_levels = _gla_tables(C)
    ms = jnp.asarray(ms_np, BF16)
    lvl = jnp.asarray(lvl_np)
    hps = heads_per_step
    assert n_heads % hps == 0
    ng = n_heads // hps
    v0 = 2 * n_heads * dk // (hps * dv)
    r0 = v0 + ng
    kern = functools.partial(_gla_kernel, C=C, n_levels=n_levels, seq_len=seq_len, dk=dk, dv=dv, heads=hps)
    R = ms_np.shape[0]
    return pl.pallas_call(
        kern,
        grid=(B, ng, Lp // C),
        in_specs=[pl.BlockSpec((None, C, hps * dk), lambda b, hd, c: (b, c, hd)),
                  pl.BlockSpec((None, C, hps * dk), lambda b, hd, c: (b, c, ng + hd)),
                  pl.BlockSpec((None, C, hps * dv), lambda b, hd, c: (b, c, v0 + hd)),
                  pl.BlockSpec((None, C, hps * dv), lambda b, hd, c: (b, c, r0 + hd)),
                  pl.BlockSpec((None, C, LANES), lambda b, hd, c: (b, c, 0)),
                  pl.BlockSpec((None, LANES, hps * dk), lambda b, hd, c: (layer, 0, hd)),
                  pl.BlockSpec((None, 1, hps * dk), lambda b, hd, c: (layer, 0, hd)),
                  pl.BlockSpec((None, 1, dv), lambda b, hd, c: (layer, 0, 0)),
                  pl.BlockSpec((None, None, hps, dk, dv), lambda b, hd, c: (s0_layer, b, hd, 0, 0)),
                  pl.BlockSpec((R, C), lambda b, hd, c: (0, 0)),
                  pl.BlockSpec((C, C), lambda b, hd, c: (0, 0))],
        out_specs=[pl.BlockSpec((None, C, hps * dv), lambda b, hd, c: (b, c, hd)),
                   pl.BlockSpec((None, hps, dk, dv), lambda b, hd, c: (b, hd, 0, 0))],
        out_shape=[jax.ShapeDtypeStruct((B, Lp, n_heads * dv), BF16),
                   jax.ShapeDtypeStruct((B, n_heads, dk, dv), F32)],
        scratch_shapes=[pltpu.VMEM((hps, dv, dk), F32)],
        compiler_params=_params("parallel", "parallel", "arbitrary"),
        name="gla_core",
    )(h, h, h, h, gl, w_gate2, b_gate, norm_gain, s0, ms, lvl)


def _rope_tables(pos):
    half = ROT_DIM // 2
    inv = jnp.power(ROPE_THETA, -jnp.arange(half, dtype=F32) * (2.0 / ROT_DIM))
    ang = pos.astype(F32)[:, None] * inv[None, :]
    cos, sin = jnp.cos(ang), jnp.sin(ang)
    n = pos.shape[0]
    cos_t = jnp.concatenate([cos, cos, jnp.ones((n, HEAD_DIM - ROT_DIM), F32)], axis=1)
    sin_t = jnp.concatenate([-sin, sin, jnp.zeros((n, HEAD_DIM - ROT_DIM), F32)], axis=1)
    return cos_t, sin_t


def _qkv_epilogue(product, o_ref, gain_ref, cos_ref, sin_ref, *, tn, section, head_major, row_chunk):
    j = pl.program_id(0)
    kind = (j * tn // section) % 3
    tm = cos_ref.shape[0]
    chunks = [slice(r, r + row_chunk) for r in range(0, tm, row_chunk)]

    def put(rows, hh, val):
        if head_major:
            o_ref[hh, rows, :] = val
        else:
            o_ref[rows, hh * HEAD_DIM:(hh + 1) * HEAD_DIM] = val

    @pl.when(kind == 2)
    def _():
        for rows in chunks:
            acc = product(rows)
            for hh in range(tn // HEAD_DIM):
                put(rows, hh, acc[:, hh * HEAD_DIM:(hh + 1) * HEAD_DIM])

    @pl.when(kind != 2)
    def _():
        half = ROT_DIM // 2
        gain = gain_ref[...]
        lane = lax.broadcasted_iota(jnp.int32, (row_chunk, HEAD_DIM), 1)
        for rows in chunks:
            acc = product(rows)
            cos_t = cos_ref[rows, :]
            sin_t = sin_ref[rows, :]
            for hh in range(tn // HEAD_DIM):
                x = acc[:, hh * HEAD_DIM:(hh + 1) * HEAD_DIM]
                y = x * lax.rsqrt(jnp.mean(x * x, axis=-1, keepdims=True) + EPS) * gain
                swapped = jnp.where(lane < half, pltpu.roll(y, HEAD_DIM - half, axis=1),
                                    pltpu.roll(y, half, axis=1))
                put(rows, hh, y * cos_t + swapped * sin_t)


def dsa_qkv(a, w_qkv, gains, cos_t, sin_t, layer, *, tm, tn, section, seq=None):
    M = a.shape[0]
    n_tab = cos_t.shape[0] // tm
    n_sec = w_qkv.shape[2] // section
    extra_specs = [pl.BlockSpec((None, 1, HEAD_DIM), lambda j, i: (layer * n_sec + j * tn // section, 0, 0)),
                   pl.BlockSpec((tm, HEAD_DIM), lambda j, i: (i % n_tab, 0)),
                   pl.BlockSpec((tm, HEAD_DIM), lambda j, i: (i % n_tab, 0))]
    epi = functools.partial(_qkv_epilogue, tn=tn, section=section, head_major=seq is not None,
                            row_chunk=min(256, tm))
    out_spec = out_shape = None
    if seq is not None:
        tps = seq // tm
        hpb = tn // HEAD_DIM
        out_spec = pl.BlockSpec((None, hpb, tm, HEAD_DIM), lambda j, i: (i // tps, j, i % tps, 0))
        out_shape = jax.ShapeDtypeStruct((M // seq, w_qkv.shape[2] // HEAD_DIM, seq, HEAD_DIM), F32)
    return mm_wres(a, w_qkv, layer, tm=tm, tn=tn, epilogue=epi, extra=(gains, cos_t, sin_t),
                   extra_specs=extra_specs, out_spec=out_spec, out_shape=out_shape, name="dsa_qkv")


def _rows(start, size, stride):
    if stride == 1:
        return (pl.ds(start, size), slice(None))
    return (pl.ds(start, size, stride=stride), slice(None))


def _prompt_attn_kernel(*refs, seq, merge_rows):
    qkv_refs = refs[:3 * N_GROUPS]
    out_ref = refs[3 * N_GROUPS]
    o_scr = refs[3 * N_GROUPS + 1:3 * N_GROUPS + 1 + N_GROUPS]
    l_scr = refs[3 * N_GROUPS + 1 + N_GROUPS:]
    scale = HEAD_DIM ** -0.5
    for g, (window, dil) in enumerate(DSA_GROUPS):
        q_ref, k_ref, v_ref = qkv_refs[3 * g:3 * g + 3]
        blk = window // dil
        span = blk * dil
        nb = seq // span
        qi = lax.broadcasted_iota(jnp.int32, (blk, 2 * blk), 0)
        ki = lax.broadcasted_iota(jnp.int32, (blk, 2 * blk), 1)
        both_ok = (ki >= qi) & (ki <= qi + blk)
        first_ok = (lax.broadcasted_iota(jnp.int32, (blk, blk), 1)
                    <= lax.broadcasted_iota(jnp.int32, (blk, blk), 0))
        for r in range(dil):
            for n in range(nb):
                start = r + n * span
                q = q_ref[_rows(start, blk, dil)].astype(BF16)
                if n == 0:
                    kk = k_ref[_rows(start, blk, dil)].astype(BF16)
                    vv = v_ref[_rows(start, blk, dil)].astype(BF16)
                    ok = first_ok
                else:
                    kk = k_ref[_rows(start - span, 2 * blk, dil)].astype(BF16)
                    vv = v_ref[_rows(start - span, 2 * blk, dil)].astype(BF16)
                    ok = both_ok
                s = jnp.where(ok, _nt(q, kk) * scale, -jnp.inf)
                mx = jnp.max(s, axis=-1, keepdims=True)
                p = jnp.exp(s - mx)
                den = jnp.sum(p, axis=-1, keepdims=True)
                o = jnp.dot(p.astype(BF16), vv, preferred_element_type=F32) / den
                o_scr[g][_rows(start, blk, dil)] = o
                l_scr[g][_rows(start, blk, dil)] = jnp.broadcast_to(mx + jnp.log(den), (blk, HEAD_DIM))
    for c in range(seq // merge_rows):
        sl = slice(c * merge_rows, (c + 1) * merge_rows)
        ls = [l_scr[g][sl, :] for g in range(N_GROUPS)]
        mx = functools.reduce(jnp.maximum, ls)
        es = [jnp.exp(l - mx) for l in ls]
        den = functools.reduce(lambda a, b: a + b, es)
        acc = es[0] * o_scr[0][sl, :]
        for g in range(1, N_GROUPS):
            acc = acc + es[g] * o_scr[g][sl, :]
        out_ref[sl, :] = (acc / den).astype(out_ref.dtype)


def prompt_attention(qkvh, *, n_heads):
    B, _, S, _ = qkvh.shape
    for window, dil in DSA_GROUPS:
        assert S % window == 0
    in_specs = []
    for g in range(N_GROUPS):
        for t in range(3):
            in_specs.append(pl.BlockSpec((None, None, S, HEAD_DIM),
                                         lambda b, h, g=g, t=t: (b, (g * 3 + t) * n_heads + h, 0, 0)))
    kern = functools.partial(_prompt_attn_kernel, seq=S, merge_rows=min(256, S))
    return pl.pallas_call(
        kern,
        grid=(B, n_heads),
        in_specs=in_specs,
        out_specs=pl.BlockSpec((None, S, HEAD_DIM), lambda b, h: (b, 0, h)),
        out_shape=jax.ShapeDtypeStruct((B, S, n_heads * HEAD_DIM), BF16),
        scratch_shapes=[pltpu.VMEM((S, HEAD_DIM), F32)] * (2 * N_GROUPS),
        compiler_params=_params("parallel", "parallel"),
        name="prompt_attention",
    )(*([qkvh] * (3 * N_GROUPS)))


def _kv_export_kernel(*refs, n_layers, n_heads, ts):
    out_ref = refs[2 * n_layers]
    layer = pl.program_id(0)
    for l in range(n_layers):
        k_ref, v_ref = refs[2 * l], refs[2 * l + 1]

        @pl.when(layer == l)
        def _():
            for h in range(n_heads):
                out_ref[pl.ds(h, ts, stride=2 * n_heads), :] = k_ref[h]
                out_ref[pl.ds(n_heads + h, ts, stride=2 * n_heads), :] = v_ref[h]


def kv_export(qkvh_layers, group, *, keep, n_heads, ts):
    n_layers = len(qkvh_layers)
    B, _, S, _ = qkvh_layers[0].shape
    nt = keep // ts
    t0 = (S - keep) // ts
    in_specs, args = [], []
    for l, arr in enumerate(qkvh_layers):
        for t in (1, 2):
            def idx(ll, b, i, l=l, t=t):
                bb = jnp.where(ll == l, b, jnp.where(ll < l, 0, B - 1))
                ii = jnp.where(ll == l, i, jnp.where(ll < l, 0, nt - 1))
                return (bb, group * 3 + t, t0 + ii, 0)
            in_specs.append(pl.BlockSpec((None, n_heads, ts, HEAD_DIM), idx))
            args.append(arr)
    rows = 2 * n_heads
    out = pl.pallas_call(
        functools.partial(_kv_export_kernel, n_layers=n_layers, n_heads=n_heads, ts=ts),
        grid=(n_layers, B, nt),
        in_specs=in_specs,
        out_specs=pl.BlockSpec((None, None, ts * rows, HEAD_DIM), lambda ll, b, i: (ll, b, i, 0)),
        out_shape=jax.ShapeDtypeStruct((n_layers, B, keep * rows, HEAD_DIM), F32),
        compiler_params=_params("arbitrary", "arbitrary", "arbitrary"),
        name=f"kv_export_g{group}",
    )(*args)
    return out.reshape(n_layers, B, keep, 2, n_heads, HEAD_DIM)


def _sample_attn_kernel(qkv_ref, c0_ref, c1_ref, c2_ref, out_ref, *, n_heads):
    scale = HEAD_DIM ** -0.5
    os, ls = [], []
    for g, c_ref in enumerate((c0_ref, c1_ref, c2_ref)):
        q, k_new, v_new = qkv_ref[3 * g], qkv_ref[3 * g + 1], qkv_ref[3 * g + 2]
        kc = c_ref[:, 0:n_heads, :]
        vc = c_ref[:, n_heads:2 * n_heads, :]
        s_c = jnp.sum(kc * q[None], axis=-1, keepdims=True) * scale
        s_n = jnp.sum(k_new * q, axis=-1, keepdims=True) * scale
        mx = jnp.maximum(jnp.max(s_c, axis=0), s_n)
        p_c = jnp.exp(s_c - mx[None])
        p_n = jnp.exp(s_n - mx)
        den = jnp.sum(p_c, axis=0) + p_n
        os.append((jnp.sum(p_c * vc, axis=0) + p_n * v_new) / den)
        ls.append(mx + jnp.log(den))
    mx = functools.reduce(jnp.maximum, ls)
    es = [jnp.exp(l - mx) for l in ls]
    den = functools.reduce(lambda a, b: a + b, es)
    acc = es[0] * os[0]
    for g in range(1, N_GROUPS):
        acc = acc + es[g] * os[g]
    out_ref[...] = (acc / den).astype(out_ref.dtype)


def sample_attention(qkv, caches, layer, *, n_heads):
    B = qkv.shape[0]
    rows = 2 * n_heads
    views, specs = [], []
    for (window, dil), c in zip(DSA_GROUPS, caches):
        n_buf = c.shape[2]
        assert n_buf == window and n_buf % dil == 0
        views.append(c.reshape(c.shape[0], B, n_buf // dil, dil * rows, HEAD_DIM))
        specs.append(pl.BlockSpec((None, None, n_buf // dil, rows, HEAD_DIM), lambda b: (layer, b, 0, 0, 0)))
    return pl.pallas_call(
        functools.partial(_sample_attn_kernel, n_heads=n_heads),
        grid=(B,),
        in_specs=[pl.BlockSpec((None, 3 * N_GROUPS, n_heads, HEAD_DIM), lambda b: (b, 0, 0, 0))] + specs,
        out_specs=pl.BlockSpec((None, n_heads, HEAD_DIM), lambda b: (b, 0, 0)),
        out_shape=jax.ShapeDtypeStruct((B, n_heads, HEAD_DIM), F32),
        compiler_params=_params("parallel"),
        name="sample_attention",
    )(qkv, *views)


def _cache_shift_kernel(cur_ref, nxt_ref, new_ref, out_ref):
    k = pl.program_id(1)
    R = out_ref.shape[0]
    out_ref[0:R - 1] = cur_ref[1:R]
    last = k == pl.num_programs(1) - 1

    @pl.when(last)
    def _():
        out_ref[R - 1] = new_ref[0]

    @pl.when(jnp.logical_not(last))
    def _():
        out_ref[R - 1] = nxt_ref[0]


def cache_shift(cache, new, *, block):
    NL, B, n_buf = cache.shape[:3]
    rows = cache.shape[3] * cache.shape[4]
    c4 = cache.reshape(NL * B, n_buf, rows, HEAD_DIM)
    n4 = new.reshape(NL * B, 1, rows, HEAD_DIM)
    R = min(block, n_buf)
    out = pl.pallas_call(
        _cache_shift_kernel,
        grid=(NL * B, n_buf // R),
        in_specs=[pl.BlockSpec((None, R, rows, HEAD_DIM), lambda lb, k: (lb, k, 0, 0)),
                  pl.BlockSpec((None, 1, rows, HEAD_DIM), lambda lb, k: (lb, jnp.minimum((k + 1) * R, n_buf - 1), 0, 0)),
                  pl.BlockSpec((None, 1, rows, HEAD_DIM), lambda lb, k: (lb, 0, 0, 0))],
        out_specs=pl.BlockSpec((None, R, rows, HEAD_DIM), lambda lb, k: (lb, k, 0, 0)),
        out_shape=jax.ShapeDtypeStruct(c4.shape, c4.dtype),
        compiler_params=_params("parallel", "parallel"),
        name="cache_shift",
    )(c4, c4, n4)
    return out.reshape(cache.shape)


def kernel(x_prompt, x_sample, cache_kv_w128, cache_kv_w512, cache_kv_w2048, state_gla, state_ffn_conv,
           norm_mix, norm_ffn, dsa_w_qkv, dsa_q_gain, dsa_k_gain, dsa_w_o,
           gla_w_in, gla_w_gate2, gla_b_gate, gla_norm_gain, gla_w_o,
           ffn_w_in, ffn_conv_w, ffn_conv_b, ffn_w_out):
    B, S, D = x_prompt.shape
    BS = x_sample.shape[0]
    assert x_sample.shape[1] == 1
    depth = norm_mix.shape[0]
    caches = (cache_kv_w128, cache_kv_w512, cache_kv_w2048)
    n_gla, gla_heads, gla_dk, gla_dv = state_gla.shape[0], state_gla.shape[2], state_gla.shape[3], state_gla.shape[4]
    gla_kd, gla_vd = gla_heads * gla_dk, gla_heads * gla_dv
    dsa_heads = dsa_w_o.shape[1] // HEAD_DIM
    W = dsa_heads * HEAD_DIM
    d_ff = ffn_w_in.shape[2] // 2
    M = B * S
    R = SAMPLE_ROWS

    tm = min(1024, S)
    tn = min(512, W)
    tw = 256
    tn_out = min(2048, D)
    gla_chunk = min(128, S)
    gla_hps = min(4, gla_heads)

    xp = x_prompt.reshape(M, D)
    xs = jnp.pad(x_sample.reshape(BS, D), ((0, R - BS), (0, 0)))

    norm_mix3 = norm_mix.reshape(depth, 1, D)
    norm_ffn3 = norm_ffn.reshape(depth, 1, D)
    conv_b3 = ffn_conv_b.reshape(depth, 1, d_ff)
    n_main = 2 * gla_kd + 2 * gla_vd
    w_gl = jnp.pad(gla_w_in[:, :, n_main:], ((0, 0), (0, 0), (0, LANES - GLA_GATE_RANK)))
    w_gate2 = jnp.pad(gla_w_gate2, ((0, 0), (0, LANES - GLA_GATE_RANK), (0, 0)))
    b_gate3 = gla_b_gate.reshape(n_gla, 1, gla_kd)
    gla_ng3 = gla_norm_gain.reshape(n_gla, 1, gla_dv)
    ones = jnp.ones_like(dsa_q_gain)
    qk_gains = jnp.stack([dsa_q_gain, dsa_k_gain, ones], axis=2).reshape(-1, 1, HEAD_DIM)
    cos_p, sin_p = _rope_tables(jnp.arange(S))
    cos_s, sin_s = _rope_tables(jnp.full((R,), PAST_LEN, jnp.int32))

    qkvh_layers = []
    kv_new = [[] for _ in DSA_GROUPS]
    gla_p, gla_s, conv_p, conv_s = [], [], [], []
    zero_state = jnp.zeros((1, B, gla_heads, gla_dk, gla_dv), F32)

    for i in range(depth):
        j = i // 2
        hp = rms_bf16(xp, norm_mix3, i, 256)
        hs = rms_bf16(xs, norm_mix3, i, R)
        if i % 2 == 0:
            h = mm_wres(hp, gla_w_in, j, tm=tm, tn=tn, ncols=n_main, name="gla_in")
            gl = mm_wres(hp, w_gl, j, tm=tm, tn=LANES, name="gla_gate_in")
            o, sp = gla_core(h.reshape(B, S, n_main), gl.reshape(B, S, LANES), w_gate2, b_gate3, gla_ng3,
                             zero_state, j, 0, seq_len=S, C=gla_chunk, n_heads=gla_heads, dk=gla_dk, dv=gla_dv,
                             heads_per_step=gla_hps)
            xp = mm_wres(o.reshape(M, gla_vd), gla_w_o, j, tm=tm, tn=tn, residual=xp, name="gla_out")
            h_s = mm_wres(hs, gla_w_in, j, tm=R, tn=tn, ncols=n_main, name="gla_in_s")
            gl_s = mm_wres(hs, w_gl, j, tm=R, tn=LANES, name="gla_gate_in_s")
            h_s = jnp.pad(h_s[:BS].reshape(BS, 1, n_main), ((0, 0), (0, R - 1), (0, 0)))
            gl_s = jnp.pad(gl_s[:BS].reshape(BS, 1, LANES), ((0, 0), (0, R - 1), (0, 0)))
            o_s, ss = gla_core(h_s, gl_s, w_gate2, b_gate3, gla_ng3, state_gla, j, j,
                               seq_len=1, C=R, n_heads=gla_heads, dk=gla_dk, dv=gla_dv, heads_per_step=gla_hps)
            o_s = jnp.pad(o_s[:, 0], ((0, R - BS), (0, 0)))
            xs = mm_wres(o_s, gla_w_o, j, tm=R, tn=tn, residual=xs, name="gla_out_s")
            gla_p.append(sp)
            gla_s.append(ss)
        else:
            qkvh = dsa_qkv(hp, dsa_w_qkv, qk_gains, cos_p, sin_p, j, tm=tm, tn=tn, section=W, seq=S)
            merged = prompt_attention(qkvh, n_heads=dsa_heads)
            xp = mm_wres(merged.reshape(M, W), dsa_w_o, j, tm=tm, tn=tn, residual=xp, name="dsa_out")
            qkvh_layers.append(qkvh)
            qkv_s = dsa_qkv(hs, dsa_w_qkv, qk_gains, cos_s, sin_s, j, tm=R, tn=tn, section=W)
            qkv_s = qkv_s[:BS].reshape(BS, N_GROUPS, 3, dsa_heads, HEAD_DIM)
            merged_s = sample_attention(qkv_s.reshape(BS, 3 * N_GROUPS, dsa_heads, HEAD_DIM), caches, j,
                                        n_heads=dsa_heads)
            merged_s = jnp.pad(merged_s.reshape(BS, W).astype(BF16), ((0, R - BS), (0, 0)))
            xs = mm_wres(merged_s, dsa_w_o, j, tm=R, tn=tn, residual=xs, name="dsa_out_s")
            for g in range(N_GROUPS):
                kv_new[g].append(qkv_s[:, g, 1:3])
        fp = rms_bf16(xp, norm_ffn3, i, 256)
        act, cp = ffn_in_prompt(fp, ffn_w_in, ffn_conv_w, conv_b3, i, seq=S, tm=tm, tw=tw)
        xp = mm_ktiled(act, ffn_w_out, i, xp, tm=tm, tn=tn_out, tk=tw, name="ffn_out")
        conv_p.append(cp)
        fs = rms_bf16(xs, norm_ffn3, i, R)
        st = jnp.pad(state_ffn_conv[i], ((0, R - BS), (0, 0), (0, 0)))
        act_s, g_s = ffn_in_sample(fs, ffn_w_in, ffn_conv_w, conv_b3, st[:, 0], st[:, 1], i, tw=tw)
        xs = mm_ktiled(act_s, ffn_w_out, i, xs, tm=R, tn=tn_out, tk=tw, name="ffn_out_s")
        conv_s.append(jnp.stack([state_ffn_conv[i][:, 1], g_s[:BS]], axis=1))

    kv128_p, kv512_p, kv2048_p = [
        kv_export(qkvh_layers, g, keep=min(window, S), n_heads=dsa_heads, ts=min(128, S))
        for g, (window, _) in enumerate(DSA_GROUPS)]
    kv128_s, kv512_s, kv2048_s = [cache_shift(c, jnp.stack(n, axis=0), block=256)
                                  for c, n in zip(caches, kv_new)]
    return (xp.reshape(B, S, D), xs[:BS].reshape(BS, 1, D),
            kv128_p, kv128_s, kv512_p, kv512_s, kv2048_p, kv2048_s,
            jnp.stack(gla_p, axis=0), jnp.stack(gla_s, axis=0),
            jnp.stack(conv_p, axis=0), jnp.stack(conv_s, axis=0))
```

```python
import functools
import math

import numpy as np
import jax
import jax.numpy as jnp
from jax import lax
from jax.experimental import pallas as pl
from jax.experimental.pallas import tpu as pltpu

F32 = jnp.float32
BF16 = jnp.bfloat16

EPS = 1e-6
HEAD_DIM = 128
ROT_DIM = HEAD_DIM // 4
ROPE_THETA = 500000.0
DSA_GROUPS = ((128, 1), (512, 4), (2048, 16))
N_GROUPS = len(DSA_GROUPS)
GLA_GATE_RANK = 16
GLA_GATE_TAU = 16.0
CONV_W = 3
PAST_LEN = 8192
LANES = 128
SAMPLE_ROWS = 16
VMEM_LIMIT_BYTES = 56 * 1024 * 1024


def _params(*sem):
    return pltpu.CompilerParams(dimension_semantics=sem, vmem_limit_bytes=VMEM_LIMIT_BYTES)


def _rms_kernel(x_ref, g_ref, o_ref):
    x = x_ref[...]
    y = x * lax.rsqrt(jnp.mean(x * x, axis=-1, keepdims=True) + EPS)
    o_ref[...] = (y * g_ref[...]).astype(o_ref.dtype)


def rms_bf16(x, gains, layer, tm):
    M, D = x.shape
    return pl.pallas_call(
        _rms_kernel,
        grid=(M // tm,),
        in_specs=[pl.BlockSpec((tm, D), lambda i: (i, 0)),
                  pl.BlockSpec((None, 1, D), lambda i: (layer, 0, 0))],
        out_specs=pl.BlockSpec((tm, D), lambda i: (i, 0)),
        out_shape=jax.ShapeDtypeStruct((M, D), BF16),
        compiler_params=_params("parallel"),
        name="rms_bf16",
    )(x, gains)


def _mm_wres_kernel(*refs, n_extra, n_extra_s, has_res, epilogue, epilogue_s):
    it = iter(refs)
    a_ref, w_ref = next(it), next(it)
    extra = [next(it) for _ in range(n_extra)]
    r_ref = next(it) if has_res else None
    as_ref = next(it)
    extra_s = [next(it) for _ in range(n_extra_s)]
    rs_ref = next(it) if has_res else None
    o_ref, os_ref, wb_ref = next(it), next(it), next(it)
    i = pl.program_id(1)

    def product_of(lhs_ref, res_ref):
        def product(rows=slice(None)):
            acc = jnp.dot(lhs_ref[rows, :], wb_ref[...], preferred_element_type=F32)
            if has_res:
                acc = acc + res_ref[rows, :]
            return acc
        return product

    @pl.when(i == 0)
    def _():
        wb_ref[...] = w_ref[...].astype(BF16)
        if epilogue_s is None:
            os_ref[...] = product_of(as_ref, rs_ref)().astype(os_ref.dtype)
        else:
            epilogue_s(product_of(as_ref, rs_ref), os_ref, *extra_s)

    if epilogue is None:
        o_ref[...] = product_of(a_ref, r_ref)().astype(o_ref.dtype)
    else:
        epilogue(product_of(a_ref, r_ref), o_ref, *extra)


def mm_wres(a, a_s, w, layer, *, tm, tn, col0=0, ncols=None, out_dtype=F32, residual=None, residual_s=None,
            epilogue=None, epilogue_s=None, extra=(), extra_specs=(), extra_s=(), extra_specs_s=(),
            out_spec=None, out_shape=None, name="mm_wres"):
    M, K = a.shape
    R = a_s.shape[0]
    ncols = w.shape[2] - col0 if ncols is None else ncols
    cb0 = col0 // tn
    if out_spec is None:
        out_spec = pl.BlockSpec((tm, tn), lambda j, i: (i, j))
        out_shape = jax.ShapeDtypeStruct((M, ncols), out_dtype)
    has_res = residual is not None
    in_specs = [pl.BlockSpec((tm, K), lambda j, i: (i, 0)),
                pl.BlockSpec((None, K, tn), lambda j, i: (layer, 0, cb0 + j))]
    in_specs += list(extra_specs)
    args = [a, w, *extra]
    if has_res:
        in_specs.append(pl.BlockSpec((tm, tn), lambda j, i: (i, j)))
        args.append(residual)
    in_specs.append(pl.BlockSpec((R, K), lambda j, i: (0, 0)))
    in_specs += list(extra_specs_s)
    args += [a_s, *extra_s]
    if has_res:
        in_specs.append(pl.BlockSpec((R, tn), lambda j, i: (0, j)))
        args.append(residual_s)
    kern = functools.partial(_mm_wres_kernel, n_extra=len(extra), n_extra_s=len(extra_s), has_res=has_res,
                             epilogue=epilogue, epilogue_s=epilogue_s)
    return pl.pallas_call(
        kern,
        grid=(ncols // tn, M // tm),
        in_specs=in_specs,
        out_specs=[out_spec, pl.BlockSpec((R, tn), lambda j, i: (0, j))],
        out_shape=[out_shape, jax.ShapeDtypeStruct((R, ncols), out_dtype)],
        scratch_shapes=[pltpu.VMEM((K, tn), BF16)],
        compiler_params=_params("parallel", "arbitrary"),
        name=name,
    )(*args)


def _mm_fullk_kernel(a_ref, w_ref, r_ref, as_ref, rs_ref, o_ref, os_ref):
    wb = w_ref[...].astype(BF16)
    o_ref[...] = r_ref[...] + jnp.dot(a_ref[...], wb, preferred_element_type=F32)

    @pl.when(pl.program_id(0) == 0)
    def _():
        os_ref[...] = rs_ref[...] + jnp.dot(as_ref[...], wb, preferred_element_type=F32)

    @pl.when(pl.program_id(0) > 0)
    def _():
        os_ref[...] = jnp.zeros_like(os_ref)


def mm_fullk(a, a_s, w, layer, residual, residual_s, *, tm, tn, name="mm_fullk"):
    M, K = a.shape
    R = a_s.shape[0]
    N = w.shape[2]
    out, out_s = pl.pallas_call(
        _mm_fullk_kernel,
        grid=(M // tm, N // tn),
        in_specs=[pl.BlockSpec((tm, K), lambda i, j: (i, 0), pipeline_mode=pl.Buffered(1)),
                  pl.BlockSpec((None, K, tn), lambda i, j: (layer, 0, j)),
                  pl.BlockSpec((tm, tn), lambda i, j: (i, j)),
                  pl.BlockSpec((R, K), lambda i, j: (0, 0)),
                  pl.BlockSpec((R, tn), lambda i, j: (0, j))],
        out_specs=[pl.BlockSpec((tm, tn), lambda i, j: (i, j)),
                   pl.BlockSpec((None, R, tn), lambda i, j: (i, 0, j))],
        out_shape=[jax.ShapeDtypeStruct((M, N), F32), jax.ShapeDtypeStruct((M // tm, R, N), F32)],
        compiler_params=_params("arbitrary", "arbitrary"),
        name=name,
    )(a, w, residual, a_s, residual_s)
    return out, out_s[0]


def _silu(x):
    return x * jax.nn.sigmoid(x)


def _ffn_in_kernel(a_ref, wg_ref, wu_ref, cw_ref, cb_ref, as_ref, s0_ref, s1_ref,
                   act_ref, st_ref, acts_ref, gs_ref, wgb_ref, wub_ref, carry_ref, *, tiles_per_seq):
    i = pl.program_id(1)
    cw = cw_ref[...]
    cb = cb_ref[...]

    @pl.when(i == 0)
    def _():
        wgb_ref[...] = wg_ref[...].astype(BF16)
        wub_ref[...] = wu_ref[...].astype(BF16)
        a_s = as_ref[...]
        g_s = jnp.dot(a_s, wgb_ref[...], preferred_element_type=F32)
        u_s = jnp.dot(a_s, wub_ref[...], preferred_element_type=F32)
        c_s = cb + cw[0:1] * s0_ref[...] + cw[1:2] * s1_ref[...] + cw[2:3] * g_s
        acts_ref[...] = (_silu(c_s) * u_s).astype(acts_ref.dtype)
        gs_ref[...] = g_s

    @pl.when(i % tiles_per_seq == 0)
    def _():
        carry_ref[...] = jnp.zeros_like(carry_ref)

    a = a_ref[...]
    g = jnp.dot(a, wgb_ref[...], preferred_element_type=F32)
    u = jnp.dot(a, wub_ref[...], preferred_element_type=F32)
    tm = g.shape[0]
    prev = carry_ref[...]
    row = lax.broadcasted_iota(jnp.int32, g.shape, 0)
    g1 = jnp.where(row == 0, prev[7:8], pltpu.roll(g, 1, axis=0))
    g2 = jnp.where(row == 0, prev[6:7],
                   jnp.where(row == 1, prev[7:8], pltpu.roll(g, 2, axis=0)))
    c = cb + cw[0:1] * g2 + cw[1:2] * g1 + cw[2:3] * g
    act_ref[...] = (_silu(c) * u).astype(act_ref.dtype)
    carry_ref[...] = g[tm - 8:tm]
    st_ref[...] = g[tm - (CONV_W - 1):tm]


def ffn_in(a, a_s, w_in, conv_w, conv_b, st0, st1, layer, *, seq, tm, tw):
    M, K = a.shape
    R = a_s.shape[0]
    d_ff = w_in.shape[2] // 2
    nb = M // seq
    tps = seq // tm
    ub0 = d_ff // tw
    kern = functools.partial(_ffn_in_kernel, tiles_per_seq=tps)
    return pl.pallas_call(
        kern,
        grid=(d_ff // tw, M // tm),
        in_specs=[pl.BlockSpec((tm, K), lambda j, i: (i, 0)),
                  pl.BlockSpec((None, K, tw), lambda j, i: (layer, 0, j)),
                  pl.BlockSpec((None, K, tw), lambda j, i: (layer, 0, ub0 + j)),
                  pl.BlockSpec((None, CONV_W, tw), lambda j, i: (layer, 0, j)),
                  pl.BlockSpec((None, 1, tw), lambda j, i: (layer, 0, j)),
                  pl.BlockSpec((R, K), lambda j, i: (0, 0)),
                  pl.BlockSpec((R, tw), lambda j, i: (0, j)),
                  pl.BlockSpec((R, tw), lambda j, i: (0, j))],
        out_specs=[pl.BlockSpec((tm, tw), lambda j, i: (i, j)),
                   pl.BlockSpec((None, CONV_W - 1, tw), lambda j, i: (i // tps, 0, j)),
                   pl.BlockSpec((R, tw), lambda j, i: (0, j)),
                   pl.BlockSpec((R, tw), lambda j, i: (0, j))],
        out_shape=[jax.ShapeDtypeStruct((M, d_ff), BF16),
                   jax.ShapeDtypeStruct((nb, CONV_W - 1, d_ff), F32),
                   jax.ShapeDtypeStruct((R, d_ff), BF16),
                   jax.ShapeDtypeStruct((R, d_ff), F32)],
        scratch_shapes=[pltpu.VMEM((K, tw), BF16), pltpu.VMEM((K, tw), BF16),
                        pltpu.VMEM((8, tw), F32)],
        compiler_params=_params("parallel", "arbitrary"),
        name="ffn_in",
    )(a, w_in, w_in, conv_w, conv_b, a_s, st0, st1)


def _gla_tables(C):
    n_levels = int(math.log2(C))
    assert 1 << n_levels == C
    i = np.arange(C)[:, None]
    t = np.arange(C)[None, :]
    mats = [t <= i, t > i]
    level = np.full((C, C), -1, np.int32)
    level[np.arange(C), np.arange(C)] = n_levels
    for lv in range(n_levels):
        s = 1 << lv
        mid = (i // (2 * s)) * (2 * s) + s
        upper = i >= mid
        mats.append((upper & (t >= mid) & (t <= i)) | (~upper & (t > i) & (t < mid)))
        same = (i // (2 * s)) == (t // (2 * s))
        own = same & upper & ~((t % (2 * s)) >= s)
        level[own] = lv
    return np.concatenate(mats, axis=0).astype(np.float32), level, n_levels


def _nt(a, b):
    return lax.dot_general(a, b, (((1,), (1,)), ((), ())), preferred_element_type=F32)


def _tn(a, b):
    return lax.dot_general(a, b, (((0,), (0,)), ((), ())), preferred_element_type=F32)


def _gla_kernel(q_ref, k_ref, v_ref, r_ref, gl_ref, w2_ref, bg_ref, ng_ref, s0_ref, ms_ref, lvl_ref,
                o_ref, sout_ref, st_ref, *, C, n_levels, seq_len, dk, dv, heads):
    c = pl.program_id(2)
    nc = pl.num_programs(2)

    @pl.when(c == 0)
    def _():
        for hh in range(heads):
            st_ref[hh] = s0_ref[hh].T

    glb = gl_ref[...].astype(BF16)
    ms = ms_ref[...]
    lvl = lvl_ref[...]
    for hh in range(heads):
        ks = slice(hh * dk, (hh + 1) * dk)
        vs = slice(hh * dv, (hh + 1) * dv)
        z = jnp.dot(glb, w2_ref[:, ks].astype(BF16), preferred_element_type=F32) + bg_ref[:, ks]
        g = -(jnp.maximum(-z, 0.0) + jnp.log1p(jnp.exp(-jnp.abs(z)))) / GLA_GATE_TAU
        if seq_len % C:
            row = c * C + lax.broadcasted_iota(jnp.int32, g.shape, 0)
            g = jnp.where(row < seq_len, g, 0.0)
        g_hi = g.astype(BF16)
        g_lo = (g - g_hi.astype(F32)).astype(BF16)
        ex = jnp.dot(ms, g_hi, preferred_element_type=F32) + jnp.dot(ms, g_lo, preferred_element_type=F32)
        e = jnp.exp(ex)

        q = q_ref[:, ks] * (dk ** -0.5)
        k = k_ref[:, ks]
        vb = v_ref[:, vs].astype(BF16)
        e_b = e[0:C]
        e_end = e[C:2 * C]
        st = st_ref[hh]
        o = _nt((q * e_b).astype(BF16), st.astype(BF16))
        a = jnp.where(lvl == n_levels, _nt(q.astype(BF16), k.astype(BF16)), 0.0)
        for lv in range(n_levels):
            e_l = e[(2 + lv) * C:(3 + lv) * C]
            a = jnp.where(lvl == lv, _nt((q * e_l).astype(BF16), (k * e_l).astype(BF16)), a)
        o = o + jnp.dot(a.astype(BF16), vb, preferred_element_type=F32)
        decay_end = e_b[C - 1:C]
        st_ref[hh] = st * decay_end + _tn(vb, (k * e_end).astype(BF16))

        y = o * lax.rsqrt(jnp.mean(o * o, axis=-1, keepdims=True) + EPS) * ng_ref[...]
        o_ref[:, vs] = (y * _silu(r_ref[:, vs])).astype(o_ref.dtype)

    @pl.when(c == nc - 1)
    def _():
        for hh in range(heads):
            sout_ref[hh] = st_ref[hh].T


def gla_core(h, gl, w_gate2, b_gate, norm_gain, s0, layer, s0_layer, *, seq_len, C, n_heads, dk, dv,
             heads_per_step):
    B, Lp, _ = h.shape
    ms_np, lvl_np, n_levels = _gla_tables(C)
    ms = jnp.asarray(ms_np, BF16)
    lvl = jnp.asarray(lvl_np)
    hps = heads_per_step
    assert n_heads % hps == 0
    ng = n_heads // hps
    v0 = 2 * n_heads * dk // (hps * dv)
    r0 = v0 + ng
    kern = functools.partial(_gla_kernel, C=C, n_levels=n_levels, seq_len=seq_len, dk=dk, dv=dv, heads=hps)
    R = ms_np.shape[0]
    return pl.pallas_call(
        kern,
        grid=(B, ng, Lp // C),
        in_specs=[pl.BlockSpec((None, C, hps * dk), lambda b, hd, c: (b, c, hd)),
                  pl.BlockSpec((None, C, hps * dk), lambda b, hd, c: (b, c, ng + hd)),
                  pl.BlockSpec((None, C, hps * dv), lambda b, hd, c: (b, c, v0 + hd)),
                  pl.BlockSpec((None, C, hps * dv), lambda b, hd, c: (b, c, r0 + hd)),
                  pl.BlockSpec((None, C, LANES), lambda b, hd, c: (b, c, 0)),
                  pl.BlockSpec((None, LANES, hps * dk), lambda b, hd, c: (layer, 0, hd)),
                  pl.BlockSpec((None, 1, hps * dk), lambda b, hd, c: (layer, 0, hd)),
                  pl.BlockSpec((None, 1, dv), lambda b, hd, c: (layer, 0, 0)),
                  pl.BlockSpec((None, None, hps, dk, dv), lambda b, hd, c: (s0_layer, b, hd, 0, 0)),
                  pl.BlockSpec((R, C), lambda b, hd, c: (0, 0)),
                  pl.BlockSpec((C, C), lambda b, hd, c: (0, 0))],
        out_specs=[pl.BlockSpec((None, C, hps * dv), lambda b, hd, c: (b, c, hd)),
                   pl.BlockSpec((None, hps, dk, dv), lambda b, hd, c: (b, hd, 0, 0))],
        out_shape=[jax.ShapeDtypeStruct((B, Lp, n_heads * dv), BF16),
                   jax.ShapeDtypeStruct((B, n_heads, dk, dv), F32)],
        scratch_shapes=[pltpu.VMEM((hps, dv, dk), F32)],
        compiler_params=_params("parallel", "parallel", "arbitrary"),
        name="gla_core",
    )(h, h, h, h, gl, w_gate2, b_gate, norm_gain, s0, ms, lvl)


def _rope_tables(pos):
    half = ROT_DIM // 2
    inv = jnp.power(ROPE_THETA, -jnp.arange(half, dtype=F32) * (2.0 / ROT_DIM))
    ang = pos.astype(F32)[:, None] * inv[None, :]
    cos, sin = jnp.cos(ang), jnp.sin(ang)
    n = pos.shape[0]
    cos_t = jnp.concatenate([cos, cos, jnp.ones((n, HEAD_DIM - ROT_DIM), F32)], axis=1)
    sin_t = jnp.concatenate([-sin, sin, jnp.zeros((n, HEAD_DIM - ROT_DIM), F32)], axis=1)
    return cos_t, sin_t


def _qkv_epilogue(product, o_ref, gain_ref, cos_ref, sin_ref, *, tn, section, head_major, row_chunk):
    j = pl.program_id(0)
    kind = (j * tn // section) % 3
    tm = cos_ref.shape[0]
    chunks = [slice(r, r + row_chunk) for r in range(0, tm, row_chunk)]

    def put(rows, hh, val):
        if head_major:
            o_ref[hh, rows, :] = val
        else:
            o_ref[rows, hh * HEAD_DIM:(hh + 1) * HEAD_DIM] = val

    @pl.when(kind == 2)
    def _():
        for rows in chunks:
            acc = product(rows)
            for hh in range(tn // HEAD_DIM):
                put(rows, hh, acc[:, hh * HEAD_DIM:(hh + 1) * HEAD_DIM])

    @pl.when(kind != 2)
    def _():
        half = ROT_DIM // 2
        gain = gain_ref[...]
        lane = lax.broadcasted_iota(jnp.int32, (row_chunk, HEAD_DIM), 1)
        for rows in chunks:
            acc = product(rows)
            cos_t = cos_ref[rows, :]
            sin_t = sin_ref[rows, :]
            for hh in range(tn // HEAD_DIM):
                x = acc[:, hh * HEAD_DIM:(hh + 1) * HEAD_DIM]
                y = x * lax.rsqrt(jnp.mean(x * x, axis=-1, keepdims=True) + EPS) * gain
                swapped = jnp.where(lane < half, pltpu.roll(y, HEAD_DIM - half, axis=1),
                                    pltpu.roll(y, half, axis=1))
                put(rows, hh, y * cos_t + swapped * sin_t)


def dsa_qkv(a, a_s, w_qkv, gains, rope, rope_s, layer, *, tm, tn, section, seq):
    M = a.shape[0]
    R = a_s.shape[0]
    tps = seq // tm
    n_sec = w_qkv.shape[2] // section
    gain_spec = pl.BlockSpec((None, 1, HEAD_DIM), lambda j, i: (layer * n_sec + j * tn // section, 0, 0))
    extra_specs = [gain_spec,
                   pl.BlockSpec((tm, HEAD_DIM), lambda j, i: (i % tps, 0)),
                   pl.BlockSpec((tm, HEAD_DIM), lambda j, i: (i % tps, 0))]
    extra_specs_s = [gain_spec,
                     pl.BlockSpec((R, HEAD_DIM), lambda j, i: (0, 0)),
                     pl.BlockSpec((R, HEAD_DIM), lambda j, i: (0, 0))]
    epi = functools.partial(_qkv_epilogue, tn=tn, section=section, head_major=True, row_chunk=min(256, tm))
    epi_s = functools.partial(_qkv_epilogue, tn=tn, section=section, head_major=False, row_chunk=R)
    hpb = tn // HEAD_DIM
    out_spec = pl.BlockSpec((None, hpb, tm, HEAD_DIM), lambda j, i: (i // tps, j, i % tps, 0))
    out_shape = jax.ShapeDtypeStruct((M // seq, w_qkv.shape[2] // HEAD_DIM, seq, HEAD_DIM), F32)
    return mm_wres(a, a_s, w_qkv, layer, tm=tm, tn=tn, epilogue=epi, epilogue_s=epi_s,
                   extra=(gains, *rope), extra_specs=extra_specs,
                   extra_s=(gains, *rope_s), extra_specs_s=extra_specs_s,
                   out_spec=out_spec, out_shape=out_shape, name="dsa_qkv")


def _rows(start, size, stride):
    if stride == 1:
        return (pl.ds(start, size), slice(None))
    return (pl.ds(start, size, stride=stride), slice(None))


def _prompt_attn_kernel(*refs, seq, merge_rows):
    qkv_refs = refs[:3 * N_GROUPS]
    out_ref = refs[3 * N_GROUPS]
    o_scr = refs[3 * N_GROUPS + 1:3 * N_GROUPS + 1 + N_GROUPS]
    l_scr = refs[3 * N_GROUPS + 1 + N_GROUPS:]
    scale = HEAD_DIM ** -0.5
    for g, (window, dil) in enumerate(DSA_GROUPS):
        q_ref, k_ref, v_ref = qkv_refs[3 * g:3 * g + 3]
        blk = window // dil
        span = blk * dil
        nb = seq // span
        qi = lax.broadcasted_iota(jnp.int32, (blk, 2 * blk), 0)
        ki = lax.broadcasted_iota(jnp.int32, (blk, 2 * blk), 1)
        both_ok = (ki >= qi) & (ki <= qi + blk)
        first_ok = (lax.broadcasted_iota(jnp.int32, (blk, blk), 1)
                    <= lax.broadcasted_iota(jnp.int32, (blk, blk), 0))
        for r in range(dil):
            for n in range(nb):
                start = r + n * span
                q = q_ref[_rows(start, blk, dil)].astype(BF16)
                if n == 0:
                    kk = k_ref[_rows(start, blk, dil)].astype(BF16)
                    vv = v_ref[_rows(start, blk, dil)].astype(BF16)
                    ok = first_ok
                else:
                    kk = k_ref[_rows(start - span, 2 * blk, dil)].astype(BF16)
                    vv = v_ref[_rows(start - span, 2 * blk, dil)].astype(BF16)
                    ok = both_ok
                s = jnp.where(ok, _nt(q, kk) * scale, -jnp.inf)
                mx = jnp.max(s, axis=-1, keepdims=True)
                p = jnp.exp(s - mx)
                den = jnp.sum(p, axis=-1, keepdims=True)
                o = jnp.dot(p.astype(BF16), vv, preferred_element_type=F32) / den
                o_scr[g][_rows(start, blk, dil)] = o
                l_scr[g][_rows(start, blk, dil)] = jnp.broadcast_to(mx + jnp.log(den), (blk, HEAD_DIM))
    for c in range(seq // merge_rows):
        sl = slice(c * merge_rows, (c + 1) * merge_rows)
        ls = [l_scr[g][sl, :] for g in range(N_GROUPS)]
        mx = functools.reduce(jnp.maximum, ls)
        es = [jnp.exp(l - mx) for l in ls]
        den = functools.reduce(lambda a, b: a + b, es)
        acc = es[0] * o_scr[0][sl, :]
        for g in range(1, N_GROUPS):
            acc = acc + es[g] * o_scr[g][sl, :]
        out_ref[sl, :] = (acc / den).astype(out_ref.dtype)


def prompt_attention(qkvh, *, n_heads):
    B, _, S, _ = qkvh.shape
    for window, dil in DSA_GROUPS:
        assert S % window == 0
    in_specs = []
    for g in range(N_GROUPS):
        for t in range(3):
            in_specs.append(pl.BlockSpec((None, None, S, HEAD_DIM),
                                         lambda b, h, g=g, t=t: (b, (g * 3 + t) * n_heads + h, 0, 0)))
    kern = functools.partial(_prompt_attn_kernel, seq=S, merge_rows=min(256, S))
    return pl.pallas_call(
        kern,
        grid=(B, n_heads),
        in_specs=in_specs,
        out_specs=pl.BlockSpec((None, S, HEAD_DIM), lambda b, h: (b, 0, h)),
        out_shape=jax.ShapeDtypeStruct((B, S, n_heads * HEAD_DIM), BF16),
        scratch_shapes=[pltpu.VMEM((S, HEAD_DIM), F32)] * (2 * N_GROUPS),
        compiler_params=_params("parallel", "parallel"),
        name="prompt_attention",
    )(*([qkvh] * (3 * N_GROUPS)))


def _kv_export_kernel(*refs, n_layers, n_heads, ts):
    out_ref = refs[2 * n_layers]
    layer = pl.program_id(0)
    for l in range(n_layers):
        k_ref, v_ref = refs[2 * l], refs[2 * l + 1]

        @pl.when(layer == l)
        def _():
            for h in range(n_heads):
                out_ref[pl.ds(h, ts, stride=2 * n_heads), :] = k_ref[h]
                out_ref[pl.ds(n_heads + h, ts, stride=2 * n_heads), :] = v_ref[h]


def kv_export(qkvh_layers, group, *, keep, n_heads, ts):
    n_layers = len(qkvh_layers)
    B, _, S, _ = qkvh_layers[0].shape
    nt = keep // ts
    t0 = (S - keep) // ts
    in_specs, args = [], []
    for l, arr in enumerate(qkvh_layers):
        for t in (1, 2):
            def idx(ll, b, i, l=l, t=t):
                bb = jnp.where(ll == l, b, jnp.where(ll < l, 0, B - 1))
                ii = jnp.where(ll == l, i, jnp.where(ll < l, 0, nt - 1))
                return (bb, group * 3 + t, t0 + ii, 0)
            in_specs.append(pl.BlockSpec((None, n_heads, ts, HEAD_DIM), idx))
            args.append(arr)
    rows = 2 * n_heads
    out = pl.pallas_call(
        functools.partial(_kv_export_kernel, n_layers=n_layers, n_heads=n_heads, ts=ts),
        grid=(n_layers, B, nt),
        in_specs=in_specs,
        out_specs=pl.BlockSpec((None, None, ts * rows, HEAD_DIM), lambda ll, b, i: (ll, b, i, 0)),
        out_shape=jax.ShapeDtypeStruct((n_layers, B, keep * rows, HEAD_DIM), F32),
        compiler_params=_params("arbitrary", "arbitrary", "arbitrary"),
        name=f"kv_export_g{group}",
    )(*args)
    return out.reshape(n_layers, B, keep, 2, n_heads, HEAD_DIM)


def _sample_attn_kernel(qkv_ref, c0_ref, c1_ref, c2_ref, out_ref, *, n_heads):
    scale = HEAD_DIM ** -0.5
    os, ls = [], []
    for g, c_ref in enumerate((c0_ref, c1_ref, c2_ref)):
        q, k_new, v_new = qkv_ref[3 * g], qkv_ref[3 * g + 1], qkv_ref[3 * g + 2]
        kc = c_ref[:, 0:n_heads, :]
        vc = c_ref[:, n_heads:2 * n_heads, :]
        s_c = jnp.sum(kc * q[None], axis=-1, keepdims=True) * scale
        s_n = jnp.sum(k_new * q, axis=-1, keepdims=True) * scale
        mx = jnp.maximum(jnp.max(s_c, axis=0), s_n)
        p_c = jnp.exp(s_c - mx[None])
        p_n = jnp.exp(s_n - mx)
        den = jnp.sum(p_c, axis=0) + p_n
        os.append((jnp.sum(p_c * vc, axis=0) + p_n * v_new) / den)
        ls.append(mx + jnp.log(den))
    mx = functools.reduce(jnp.maximum, ls)
    es = [jnp.exp(l - mx) for l in ls]
    den = functools.reduce(lambda a, b: a + b, es)
    acc = es[0] * os[0]
    for g in range(1, N_GROUPS):
        acc = acc + es[g] * os[g]
    out_ref[...] = (acc / den).astype(out_ref.dtype)


def sample_attention(qkv, caches, layer, *, n_heads):
    B = qkv.shape[0]
    rows = 2 * n_heads
    views, specs = [], []
    for (window, dil), c in zip(DSA_GROUPS, caches):
        n_buf = c.shape[2]
        assert n_buf == window and n_buf % dil == 0
        views.append(c.reshape(c.shape[0], B, n_buf // dil, dil * rows, HEAD_DIM))
        specs.append(pl.BlockSpec((None, None, n_buf // dil, rows, HEAD_DIM), lambda b: (layer, b, 0, 0, 0)))
    return pl.pallas_call(
        functools.partial(_sample_attn_kernel, n_heads=n_heads),
        grid=(B,),
        in_specs=[pl.BlockSpec((None, 3 * N_GROUPS, n_heads, HEAD_DIM), lambda b: (b, 0, 0, 0))] + specs,
        out_specs=pl.BlockSpec((None, n_heads, HEAD_DIM), lambda b: (b, 0, 0)),
        out_shape=jax.ShapeDtypeStruct((B, n_heads, HEAD_DIM), F32),
        compiler_params=_params("parallel"),
        name="sample_attention",
    )(qkv, *views)


def _cache_shift_kernel(cur_ref, nxt_ref, new_ref, out_ref):
    k = pl.program_id(1)
    R = out_ref.shape[0]
    out_ref[0:R - 1] = cur_ref[1:R]
    last = k == pl.num_programs(1) - 1

    @pl.when(last)
    def _():
        out_ref[R - 1] = new_ref[0]

    @pl.when(jnp.logical_not(last))
    def _():
        out_ref[R - 1] = nxt_ref[0]


def cache_shift(cache, new, *, block):
    NL, B, n_buf = cache.shape[:3]
    rows = cache.shape[3] * cache.shape[4]
    c4 = cache.reshape(NL * B, n_buf, rows, HEAD_DIM)
    n4 = new.reshape(NL * B, 1, rows, HEAD_DIM)
    R = min(block, n_buf)
    out = pl.pallas_call(
        _cache_shift_kernel,
        grid=(NL * B, n_buf // R),
        in_specs=[pl.BlockSpec((None, R, rows, HEAD_DIM), lambda lb, k: (lb, k, 0, 0)),
                  pl.BlockSpec((None, 1, rows, HEAD_DIM), lambda lb, k: (lb, jnp.minimum((k + 1) * R, n_buf - 1), 0, 0)),
                  pl.BlockSpec((None, 1, rows, HEAD_DIM), lambda lb, k: (lb, 0, 0, 0))],
        out_specs=pl.BlockSpec((None, R, rows, HEAD_DIM), lambda lb, k: (lb, k, 0, 0)),
        out_shape=jax.ShapeDtypeStruct(c4.shape, c4.dtype),
        compiler_params=_params("parallel", "parallel"),
        name="cache_shift",
    )(c4, c4, n4)
    return out.reshape(cache.shape)


def kernel(x_prompt, x_sample, cache_kv_w128, cache_kv_w512, cache_kv_w2048, state_gla, state_ffn_conv,
           norm_mix, norm_ffn, dsa_w_qkv, dsa_q_gain, dsa_k_gain, dsa_w_o,
           gla_w_in, gla_w_gate2, gla_b_gate, gla_norm_gain, gla_w_o,
           ffn_w_in, ffn_conv_w, ffn_conv_b, ffn_w_out):
    B, S, D = x_prompt.shape
    BS = x_sample.shape[0]
    assert x_sample.shape[1] == 1
    depth = norm_mix.shape[0]
    caches = (cache_kv_w128, cache_kv_w512, cache_kv_w2048)
    n_gla, gla_heads, gla_dk, gla_dv = state_gla.shape[0], state_gla.shape[2], state_gla.shape[3], state_gla.shape[4]
    gla_kd, gla_vd = gla_heads * gla_dk, gla_heads * gla_dv
    dsa_heads = dsa_w_o.shape[1] // HEAD_DIM
    W = dsa_heads * HEAD_DIM
    d_ff = ffn_w_in.shape[2] // 2
    M = B * S
    R = SAMPLE_ROWS

    tm = min(1024, S)
    tn = min(512, W)
    tw = 256
    gla_chunk = min(128, S)
    gla_hps = min(4, gla_heads)

    xp = x_prompt.reshape(M, D)
    xs = jnp.pad(x_sample.reshape(BS, D), ((0, R - BS), (0, 0)))

    norm_mix3 = norm_mix.reshape(depth, 1, D)
    norm_ffn3 = norm_ffn.reshape(depth, 1, D)
    conv_b3 = ffn_conv_b.reshape(depth, 1, d_ff)
    n_main = 2 * gla_kd + 2 * gla_vd
    w_gl = jnp.pad(gla_w_in[:, :, n_main:], ((0, 0), (0, 0), (0, LANES - GLA_GATE_RANK)))
    w_gate2 = jnp.pad(gla_w_gate2, ((0, 0), (0, LANES - GLA_GATE_RANK), (0, 0)))
    b_gate3 = gla_b_gate.reshape(n_gla, 1, gla_kd)
    gla_ng3 = gla_norm_gain.reshape(n_gla, 1, gla_dv)
    ones = jnp.ones_like(dsa_q_gain)
    qk_gains = jnp.stack([dsa_q_gain, dsa_k_gain, ones], axis=2).reshape(-1, 1, HEAD_DIM)
    rope_p = _rope_tables(jnp.arange(S))
    rope_s = _rope_tables(jnp.full((R,), PAST_LEN, jnp.int32))

    qkvh_layers = []
    kv_new = [[] for _ in DSA_GROUPS]
    gla_p, gla_s, conv_p, conv_s = [], [], [], []
    zero_state = jnp.zeros((1, B, gla_heads, gla_dk, gla_dv), F32)

    for i in range(depth):
        j = i // 2
        hp = rms_bf16(xp, norm_mix3, i, 256)
        hs = rms_bf16(xs, norm_mix3, i, R)
        if i % 2 == 0:
            h, h_s = mm_wres(hp, hs, gla_w_in, j, tm=tm, tn=tn, ncols=n_main, name="gla_in")
            gl, gl_s = mm_wres(hp, hs, w_gl, j, tm=tm, tn=LANES, name="gla_gate_in")
            o, sp = gla_core(h.reshape(B, S, n_main), gl.reshape(B, S, LANES), w_gate2, b_gate3, gla_ng3,
                             zero_state, j, 0, seq_len=S, C=gla_chunk, n_heads=gla_heads, dk=gla_dk, dv=gla_dv,
                             heads_per_step=gla_hps)
            h_s = jnp.pad(h_s[:BS].reshape(BS, 1, n_main), ((0, 0), (0, R - 1), (0, 0)))
            gl_s = jnp.pad(gl_s[:BS].reshape(BS, 1, LANES), ((0, 0), (0, R - 1), (0, 0)))
            o_s, ss = gla_core(h_s, gl_s, w_gate2, b_gate3, gla_ng3, state_gla, j, j,
                               seq_len=1, C=R, n_heads=gla_heads, dk=gla_dk, dv=gla_dv, heads_per_step=gla_hps)
            o_s = jnp.pad(o_s[:, 0], ((0, R - BS), (0, 0)))
            xp, xs = mm_wres(o.reshape(M, gla_vd), o_s, gla_w_o, j, tm=tm, tn=tn, residual=xp, residual_s=xs,
                             name="gla_out")
            gla_p.append(sp)
            gla_s.append(ss)
        else:
            qkvh, qkv_s = dsa_qkv(hp, hs, dsa_w_qkv, qk_gains, rope_p, rope_s, j, tm=tm, tn=tn, section=W, seq=S)
            merged = prompt_attention(qkvh, n_heads=dsa_heads)
            qkvh_layers.append(qkvh)
            qkv_s = qkv_s[:BS].reshape(BS, N_GROUPS, 3, dsa_heads, HEAD_DIM)
            merged_s = sample_attention(qkv_s.reshape(BS, 3 * N_GROUPS, dsa_heads, HEAD_DIM), caches, j,
                                        n_heads=dsa_heads)
            merged_s = jnp.pad(merged_s.reshape(BS, W).astype(BF16), ((0, R - BS), (0, 0)))
            xp, xs = mm_wres(merged.reshape(M, W), merged_s, dsa_w_o, j, tm=tm, tn=tn, residual=xp,
                             residual_s=xs, name="dsa_out")
            for g in range(N_GROUPS):
                kv_new[g].append(qkv_s[:, g, 1:3])
        fp = rms_bf16(xp, norm_ffn3, i, 256)
        fs = rms_bf16(xs, norm_ffn3, i, R)
        st = jnp.pad(state_ffn_conv[i], ((0, R - BS), (0, 0), (0, 0)))
        act, cp, act_s, g_s = ffn_in(fp, fs, ffn_w_in, ffn_conv_w, conv_b3, st[:, 0], st[:, 1], i,
                                     seq=S, tm=tm, tw=tw)
        xp, xs = mm_fullk(act, act_s, ffn_w_out, i, xp, xs, tm=tm, tn=tw, name="ffn_out")
        conv_p.append(cp)
        conv_s.append(jnp.stack([state_ffn_conv[i][:, 1], g_s[:BS]], axis=1))

    kv128_p, kv512_p, kv2048_p = [
        kv_export(qkvh_layers, g, keep=min(window, S), n_heads=dsa_heads, ts=min(128, S))
        for g, (window, _) in enumerate(DSA_GROUPS)]
    kv128_s, kv512_s, kv2048_s = [cache_shift(c, jnp.stack(n, axis=0), block=256)
                                  for c, n in zip(caches, kv_new)]
    return (xp.reshape(B, S, D), xs[:BS].reshape(BS, 1, D),
            kv128_p, kv128_s, kv512_p, kv512_s, kv2048_p, kv2048_s,
            jnp.stack(gla_p, axis=0), jnp.stack(gla_s, axis=0),
            jnp.stack(conv_p, axis=0), jnp.stack(conv_s, axis=0))
```

```python
import functools
import math

import numpy as np
import jax
import jax.numpy as jnp
from jax import lax
from jax.experimental import pallas as pl
from jax.experimental.pallas import tpu as pltpu

F32 = jnp.float32
BF16 = jnp.bfloat16

EPS = 1e-6
HEAD_DIM = 128
ROT_DIM = HEAD_DIM // 4
ROPE_THETA = 500000.0
DSA_GROUPS = ((128, 1), (512, 4), (2048, 16))
N_GROUPS = len(DSA_GROUPS)
GLA_GATE_RANK = 16
GLA_GATE_TAU = 16.0
CONV_W = 3
PAST_LEN = 8192
LANES = 128
SAMPLE_ROWS = 16
VMEM_LIMIT_BYTES = 56 * 1024 * 1024


def _params(*sem):
    return pltpu.CompilerParams(dimension_semantics=sem, vmem_limit_bytes=VMEM_LIMIT_BYTES)


def _rms_kernel(x_ref, g_ref, o_ref):
    x = x_ref[...]
    y = x * lax.rsqrt(jnp.mean(x * x, axis=-1, keepdims=True) + EPS)
    o_ref[...] = (y * g_ref[...]).astype(o_ref.dtype)


def rms_bf16(x, gains, layer, tm):
    M, D = x.shape
    return pl.pallas_call(
        _rms_kernel,
        grid=(M // tm,),
        in_specs=[pl.BlockSpec((tm, D), lambda i: (i, 0)),
                  pl.BlockSpec((None, 1, D), lambda i: (layer, 0, 0))],
        out_specs=pl.BlockSpec((tm, D), lambda i: (i, 0)),
        out_shape=jax.ShapeDtypeStruct((M, D), BF16),
        compiler_params=_params("parallel"),
        name="rms_bf16",
    )(x, gains)


def _mm_wres_kernel(*refs, n_extra, n_extra_s, has_res, epilogue, epilogue_s, w_transposed):
    it = iter(refs)
    a_ref, w_ref = next(it), next(it)
    extra = [next(it) for _ in range(n_extra)]
    r_ref = next(it) if has_res else None
    as_ref = next(it)
    extra_s = [next(it) for _ in range(n_extra_s)]
    rs_ref = next(it) if has_res else None
    o_ref, os_ref, wb_ref = next(it), next(it), next(it)
    i = pl.program_id(1)

    def product_of(lhs_ref, res_ref):
        def product(rows=slice(None)):
            if w_transposed:
                acc = _nt(lhs_ref[rows, :], wb_ref[...])
            else:
                acc = jnp.dot(lhs_ref[rows, :], wb_ref[...], preferred_element_type=F32)
            if has_res:
                acc = acc + res_ref[rows, :]
            return acc
        return product

    @pl.when(i == 0)
    def _():
        wb_ref[...] = w_ref[...].astype(BF16)
        if epilogue_s is None:
            os_ref[...] = product_of(as_ref, rs_ref)().astype(os_ref.dtype)
        else:
            epilogue_s(product_of(as_ref, rs_ref), os_ref, *extra_s)

    if epilogue is None:
        o_ref[...] = product_of(a_ref, r_ref)().astype(o_ref.dtype)
    else:
        epilogue(product_of(a_ref, r_ref), o_ref, *extra)


def mm_wres(a, a_s, w, layer, *, tm, tn, col0=0, ncols=None, out_dtype=F32, residual=None, residual_s=None,
            epilogue=None, epilogue_s=None, extra=(), extra_specs=(), extra_s=(), extra_specs_s=(),
            out_spec=None, out_shape=None, w_transposed=False, name="mm_wres"):
    M, K = a.shape
    R = a_s.shape[0]
    ncols = w.shape[1 if w_transposed else 2] - col0 if ncols is None else ncols
    cb0 = col0 // tn
    if w_transposed:
        w_spec = pl.BlockSpec((None, tn, K), lambda j, i: (layer, cb0 + j, 0))
    else:
        w_spec = pl.BlockSpec((None, K, tn), lambda j, i: (layer, 0, cb0 + j))
    if out_spec is None:
        out_spec = pl.BlockSpec((tm, tn), lambda j, i: (i, j))
        out_shape = jax.ShapeDtypeStruct((M, ncols), out_dtype)
    has_res = residual is not None
    in_specs = [pl.BlockSpec((tm, K), lambda j, i: (i, 0)), w_spec]
    in_specs += list(extra_specs)
    args = [a, w, *extra]
    if has_res:
        in_specs.append(pl.BlockSpec((tm, tn), lambda j, i: (i, j)))
        args.append(residual)
    in_specs.append(pl.BlockSpec((R, K), lambda j, i: (0, 0)))
    in_specs += list(extra_specs_s)
    args += [a_s, *extra_s]
    if has_res:
        in_specs.append(pl.BlockSpec((R, tn), lambda j, i: (0, j)))
        args.append(residual_s)
    kern = functools.partial(_mm_wres_kernel, n_extra=len(extra), n_extra_s=len(extra_s), has_res=has_res,
                             epilogue=epilogue, epilogue_s=epilogue_s, w_transposed=w_transposed)
    return pl.pallas_call(
        kern,
        grid=(ncols // tn, M // tm),
        in_specs=in_specs,
        out_specs=[out_spec, pl.BlockSpec((R, tn), lambda j, i: (0, j))],
        out_shape=[out_shape, jax.ShapeDtypeStruct((R, ncols), out_dtype)],
        scratch_shapes=[pltpu.VMEM((tn, K) if w_transposed else (K, tn), BF16)],
        compiler_params=_params("parallel", "arbitrary"),
        name=name,
    )(*args)


def _mm_fullk_kernel(a_ref, w_ref, r_ref, as_ref, rs_ref, o_ref, os_ref):
    wb = w_ref[...].astype(BF16)
    o_ref[...] = r_ref[...] + jnp.dot(a_ref[...], wb, preferred_element_type=F32)

    @pl.when(pl.program_id(0) == 0)
    def _():
        os_ref[...] = rs_ref[...] + jnp.dot(as_ref[...], wb, preferred_element_type=F32)

    @pl.when(pl.program_id(0) > 0)
    def _():
        os_ref[...] = jnp.zeros_like(os_ref)


def mm_fullk(a, a_s, w, layer, residual, residual_s, *, tm, tn, name="mm_fullk"):
    M, K = a.shape
    R = a_s.shape[0]
    N = w.shape[2]
    out, out_s = pl.pallas_call(
        _mm_fullk_kernel,
        grid=(M // tm, N // tn),
        in_specs=[pl.BlockSpec((tm, K), lambda i, j: (i, 0), pipeline_mode=pl.Buffered(1)),
                  pl.BlockSpec((None, K, tn), lambda i, j: (layer, 0, j)),
                  pl.BlockSpec((tm, tn), lambda i, j: (i, j)),
                  pl.BlockSpec((R, K), lambda i, j: (0, 0)),
                  pl.BlockSpec((R, tn), lambda i, j: (0, j))],
        out_specs=[pl.BlockSpec((tm, tn), lambda i, j: (i, j)),
                   pl.BlockSpec((None, R, tn), lambda i, j: (i, 0, j))],
        out_shape=[jax.ShapeDtypeStruct((M, N), F32), jax.ShapeDtypeStruct((M // tm, R, N), F32)],
        compiler_params=_params("arbitrary", "arbitrary"),
        name=name,
    )(a, w, residual, a_s, residual_s)
    return out, out_s[0]


def _silu(x):
    return x * jax.nn.sigmoid(x)


def _ffn_in_kernel(a_ref, wg_ref, wu_ref, cw_ref, cb_ref, as_ref, s0_ref, s1_ref,
                   act_ref, st_ref, acts_ref, gs_ref, wgb_ref, wub_ref, carry_ref, *, tiles_per_seq):
    i = pl.program_id(1)
    cw = cw_ref[...]
    cb = cb_ref[...]

    @pl.when(i == 0)
    def _():
        wgb_ref[...] = wg_ref[...].astype(BF16)
        wub_ref[...] = wu_ref[...].astype(BF16)
        a_s = as_ref[...]
        g_s = jnp.dot(a_s, wgb_ref[...], preferred_element_type=F32)
        u_s = jnp.dot(a_s, wub_ref[...], preferred_element_type=F32)
        c_s = cb + cw[0:1] * s0_ref[...] + cw[1:2] * s1_ref[...] + cw[2:3] * g_s
        acts_ref[...] = (_silu(c_s) * u_s).astype(acts_ref.dtype)
        gs_ref[...] = g_s

    @pl.when(i % tiles_per_seq == 0)
    def _():
        carry_ref[...] = jnp.zeros_like(carry_ref)

    a = a_ref[...]
    g = jnp.dot(a, wgb_ref[...], preferred_element_type=F32)
    u = jnp.dot(a, wub_ref[...], preferred_element_type=F32)
    tm = g.shape[0]
    prev = carry_ref[...]
    row = lax.broadcasted_iota(jnp.int32, g.shape, 0)
    g1 = jnp.where(row == 0, prev[7:8], pltpu.roll(g, 1, axis=0))
    g2 = jnp.where(row == 0, prev[6:7],
                   jnp.where(row == 1, prev[7:8], pltpu.roll(g, 2, axis=0)))
    c = cb + cw[0:1] * g2 + cw[1:2] * g1 + cw[2:3] * g
    act_ref[...] = (_silu(c) * u).astype(act_ref.dtype)
    carry_ref[...] = g[tm - 8:tm]
    st_ref[...] = g[tm - (CONV_W - 1):tm]


def ffn_in(a, a_s, w_in, conv_w, conv_b, st0, st1, layer, *, seq, tm, tw):
    M, K = a.shape
    R = a_s.shape[0]
    d_ff = w_in.shape[2] // 2
    nb = M // seq
    tps = seq // tm
    ub0 = d_ff // tw
    kern = functools.partial(_ffn_in_kernel, tiles_per_seq=tps)
    return pl.pallas_call(
        kern,
        grid=(d_ff // tw, M // tm),
        in_specs=[pl.BlockSpec((tm, K), lambda j, i: (i, 0)),
                  pl.BlockSpec((None, K, tw), lambda j, i: (layer, 0, j)),
                  pl.BlockSpec((None, K, tw), lambda j, i: (layer, 0, ub0 + j)),
                  pl.BlockSpec((None, CONV_W, tw), lambda j, i: (layer, 0, j)),
                  pl.BlockSpec((None, 1, tw), lambda j, i: (layer, 0, j)),
                  pl.BlockSpec((R, K), lambda j, i: (0, 0)),
                  pl.BlockSpec((R, tw), lambda j, i: (0, j)),
                  pl.BlockSpec((R, tw), lambda j, i: (0, j))],
        out_specs=[pl.BlockSpec((tm, tw), lambda j, i: (i, j)),
                   pl.BlockSpec((None, CONV_W - 1, tw), lambda j, i: (i // tps, 0, j)),
                   pl.BlockSpec((R, tw), lambda j, i: (0, j)),
                   pl.BlockSpec((R, tw), lambda j, i: (0, j))],
        out_shape=[jax.ShapeDtypeStruct((M, d_ff), BF16),
                   jax.ShapeDtypeStruct((nb, CONV_W - 1, d_ff), F32),
                   jax.ShapeDtypeStruct((R, d_ff), BF16),
                   jax.ShapeDtypeStruct((R, d_ff), F32)],
        scratch_shapes=[pltpu.VMEM((K, tw), BF16), pltpu.VMEM((K, tw), BF16),
                        pltpu.VMEM((8, tw), F32)],
        compiler_params=_params("parallel", "arbitrary"),
        name="ffn_in",
    )(a, w_in, w_in, conv_w, conv_b, a_s, st0, st1)


def _gla_tables(C):
    n_levels = int(math.log2(C))
    assert 1 << n_levels == C
    i = np.arange(C)[:, None]
    t = np.arange(C)[None, :]
    mats = [t <= i, t > i]
    level = np.full((C, C), -1, np.int32)
    level[np.arange(C), np.arange(C)] = n_levels
    for lv in range(n_levels):
        s = 1 << lv
        mid = (i // (2 * s)) * (2 * s) + s
        upper = i >= mid
        mats.append((upper & (t >= mid) & (t <= i)) | (~upper & (t > i) & (t < mid)))
        same = (i // (2 * s)) == (t // (2 * s))
        own = same & upper & ~((t % (2 * s)) >= s)
        level[own] = lv
    return np.concatenate(mats, axis=0).astype(np.float32), level, n_levels


def _nt(a, b):
    return lax.dot_general(a, b, (((1,), (1,)), ((), ())), preferred_element_type=F32)


def _tn(a, b):
    return lax.dot_general(a, b, (((0,), (0,)), ((), ())), preferred_element_type=F32)


def _gla_kernel(q_ref, k_ref, v_ref, r_ref, gl_ref, w2_ref, bg_ref, ng_ref, s0_ref, ms_ref, lvl_ref,
                o_ref, sout_ref, st_ref, *, C, n_levels, seq_len, dk, dv, heads):
    c = pl.program_id(2)
    nc = pl.num_programs(2)

    @pl.when(c == 0)
    def _():
        for hh in range(heads):
            st_ref[hh] = s0_ref[hh].T

    glb = gl_ref[...].astype(BF16)
    ms = ms_ref[...]
    lvl = lvl_ref[...]
    for hh in range(heads):
        ks = slice(hh * dk, (hh + 1) * dk)
        vs = slice(hh * dv, (hh + 1) * dv)
        z = jnp.dot(glb, w2_ref[:, ks].astype(BF16), preferred_element_type=F32) + bg_ref[:, ks]
        g = -(jnp.maximum(-z, 0.0) + jnp.log1p(jnp.exp(-jnp.abs(z)))) / GLA_GATE_TAU
        if seq_len % C:
            row = c * C + lax.broadcasted_iota(jnp.int32, g.shape, 0)
            g = jnp.where(row < seq_len, g, 0.0)
        g_hi = g.astype(BF16)
        g_lo = (g - g_hi.astype(F32)).astype(BF16)
        ex = jnp.dot(ms, g_hi, preferred_element_type=F32) + jnp.dot(ms, g_lo, preferred_element_type=F32)
        e = jnp.exp(ex)

        q = q_ref[:, ks] * (dk ** -0.5)
        k = k_ref[:, ks]
        vb = v_ref[:, vs].astype(BF16)
        e_b = e[0:C]
        e_end = e[C:2 * C]
        st = st_ref[hh]
        o = _nt((q * e_b).astype(BF16), st.astype(BF16))
        a = jnp.where(lvl == n_levels, _nt(q.astype(BF16), k.astype(BF16)), 0.0)
        for lv in range(n_levels):
            e_l = e[(2 + lv) * C:(3 + lv) * C]
            a = jnp.where(lvl == lv, _nt((q * e_l).astype(BF16), (k * e_l).astype(BF16)), a)
        o = o + jnp.dot(a.astype(BF16), vb, preferred_element_type=F32)
        decay_end = e_b[C - 1:C]
        st_ref[hh] = st * decay_end + _tn(vb, (k * e_end).astype(BF16))

        y = o * lax.rsqrt(jnp.mean(o * o, axis=-1, keepdims=True) + EPS) * ng_ref[...]
        o_ref[:, vs] = (y * _silu(r_ref[:, vs])).astype(o_ref.dtype)

    @pl.when(c == nc - 1)
    def _():
        for hh in range(heads):
            sout_ref[hh] = st_ref[hh].T


def gla_core(h, gl, w_gate2, b_gate, norm_gain, s0, layer, s0_layer, *, seq_len, C, n_heads, dk, dv,
             heads_per_step):
    B, Lp, _ = h.shape
    ms_np, lvl_np, n_levels = _gla_tables(C)
    ms = jnp.asarray(ms_np, BF16)
    lvl = jnp.asarray(lvl_np)
    hps = heads_per_step
    assert n_heads % hps == 0
    ng = n_heads // hps
    v0 = 2 * n_heads * dk // (hps * dv)
    r0 = v0 + ng
    kern = functools.partial(_gla_kernel, C=C, n_levels=n_levels, seq_len=seq_len, dk=dk, dv=dv, heads=hps)
    R = ms_np.shape[0]
    return pl.pallas_call(
        kern,
        grid=(B, ng, Lp // C),
        in_specs=[pl.BlockSpec((None, C, hps * dk), lambda b, hd, c: (b, c, hd)),
                  pl.BlockSpec((None, C, hps * dk), lambda b, hd, c: (b, c, ng + hd)),
                  pl.BlockSpec((None, C, hps * dv), lambda b, hd, c: (b, c, v0 + hd)),
                  pl.BlockSpec((None, C, hps * dv), lambda b, hd, c: (b, c, r0 + hd)),
                  pl.BlockSpec((None, C, LANES), lambda b, hd, c: (b, c, 0)),
                  pl.BlockSpec((None, LANES, hps * dk), lambda b, hd, c: (layer, 0, hd)),
                  pl.BlockSpec((None, 1, hps * dk), lambda b, hd, c: (layer, 0, hd)),
                  pl.BlockSpec((None, 1, dv), lambda b, hd, c: (layer, 0, 0)),
                  pl.BlockSpec((None, None, hps, dk, dv), lambda b, hd, c: (s0_layer, b, hd, 0, 0)),
                  pl.BlockSpec((R, C), lambda b, hd, c: (0, 0)),
                  pl.BlockSpec((C, C), lambda b, hd, c: (0, 0))],
        out_specs=[pl.BlockSpec((None, C, hps * dv), lambda b, hd, c: (b, c, hd)),
                   pl.BlockSpec((None, hps, dk, dv), lambda b, hd, c: (b, hd, 0, 0))],
        out_shape=[jax.ShapeDtypeStruct((B, Lp, n_heads * dv), BF16),
                   jax.ShapeDtypeStruct((B, n_heads, dk, dv), F32)],
        scratch_shapes=[pltpu.VMEM((hps, dv, dk), F32)],
        compiler_params=_params("parallel", "parallel", "arbitrary"),
        name="gla_core",
    )(h, h, h, h, gl, w_gate2, b_gate, norm_gain, s0, ms, lvl)


def _rope_tables(pos):
    half = ROT_DIM // 2
    inv = jnp.power(ROPE_THETA, -jnp.arange(half, dtype=F32) * (2.0 / ROT_DIM))
    ang = pos.astype(F32)[:, None] * inv[None, :]
    cos, sin = jnp.cos(ang), jnp.sin(ang)
    n = pos.shape[0]
    cos_t = jnp.concatenate([cos, cos, jnp.ones((n, HEAD_DIM - ROT_DIM), F32)], axis=1)
    sin_t = jnp.concatenate([-sin, sin, jnp.zeros((n, HEAD_DIM - ROT_DIM), F32)], axis=1)
    return cos_t, sin_t


def _qkv_epilogue(product, o_ref, gain_ref, cos_ref, sin_ref, *, tn, section, head_major, row_chunk):
    j = pl.program_id(0)
    kind = (j * tn // section) % 3
    tm = cos_ref.shape[0]
    chunks = [slice(r, r + row_chunk) for r in range(0, tm, row_chunk)]

    def put(rows, hh, val):
        if head_major:
            o_ref[hh, rows, :] = val
        else:
            o_ref[rows, hh * HEAD_DIM:(hh + 1) * HEAD_DIM] = val

    @pl.when(kind == 2)
    def _():
        for rows in chunks:
            acc = product(rows)
            for hh in range(tn // HEAD_DIM):
                put(rows, hh, acc[:, hh * HEAD_DIM:(hh + 1) * HEAD_DIM])

    @pl.when(kind != 2)
    def _():
        half = ROT_DIM // 2
        gain = gain_ref[...]
        lane = lax.broadcasted_iota(jnp.int32, (row_chunk, HEAD_DIM), 1)
        for rows in chunks:
            acc = product(rows)
            cos_t = cos_ref[rows, :]
            sin_t = sin_ref[rows, :]
            for hh in range(tn // HEAD_DIM):
                x = acc[:, hh * HEAD_DIM:(hh + 1) * HEAD_DIM]
                y = x * lax.rsqrt(jnp.mean(x * x, axis=-1, keepdims=True) + EPS) * gain
                swapped = jnp.where(lane < half, pltpu.roll(y, HEAD_DIM - half, axis=1),
                                    pltpu.roll(y, half, axis=1))
                put(rows, hh, y * cos_t + swapped * sin_t)


def dsa_qkv(a, a_s, w_qkv, gains, rope, rope_s, layer, *, tm, tn, section, seq):
    M = a.shape[0]
    R = a_s.shape[0]
    tps = seq // tm
    n_sec = w_qkv.shape[2] // section
    gain_spec = pl.BlockSpec((None, 1, HEAD_DIM), lambda j, i: (layer * n_sec + j * tn // section, 0, 0))
    extra_specs = [gain_spec,
                   pl.BlockSpec((tm, HEAD_DIM), lambda j, i: (i % tps, 0)),
                   pl.BlockSpec((tm, HEAD_DIM), lambda j, i: (i % tps, 0))]
    extra_specs_s = [gain_spec,
                     pl.BlockSpec((R, HEAD_DIM), lambda j, i: (0, 0)),
                     pl.BlockSpec((R, HEAD_DIM), lambda j, i: (0, 0))]
    epi = functools.partial(_qkv_epilogue, tn=tn, section=section, head_major=True, row_chunk=min(256, tm))
    epi_s = functools.partial(_qkv_epilogue, tn=tn, section=section, head_major=False, row_chunk=R)
    hpb = tn // HEAD_DIM
    out_spec = pl.BlockSpec((None, hpb, tm, HEAD_DIM), lambda j, i: (i // tps, j, i % tps, 0))
    out_shape = jax.ShapeDtypeStruct((M // seq, w_qkv.shape[2] // HEAD_DIM, seq, HEAD_DIM), F32)
    return mm_wres(a, a_s, w_qkv, layer, tm=tm, tn=tn, epilogue=epi, epilogue_s=epi_s,
                   extra=(gains, *rope), extra_specs=extra_specs,
                   extra_s=(gains, *rope_s), extra_specs_s=extra_specs_s,
                   out_spec=out_spec, out_shape=out_shape, name="dsa_qkv")


def _rows(start, size, stride):
    if stride == 1:
        return (pl.ds(start, size), slice(None))
    return (pl.ds(start, size, stride=stride), slice(None))


def _prompt_attn_kernel(*refs, seq, merge_rows):
    qkv_refs = refs[:3 * N_GROUPS]
    out_ref = refs[3 * N_GROUPS]
    o_scr = refs[3 * N_GROUPS + 1:3 * N_GROUPS + 1 + N_GROUPS]
    l_scr = refs[3 * N_GROUPS + 1 + N_GROUPS:]
    scale = HEAD_DIM ** -0.5
    for g, (window, dil) in enumerate(DSA_GROUPS):
        q_ref, k_ref, v_ref = qkv_refs[3 * g:3 * g + 3]
        blk = window // dil
        span = blk * dil
        nb = seq // span
        qi = lax.broadcasted_iota(jnp.int32, (blk, 2 * blk), 0)
        ki = lax.broadcasted_iota(jnp.int32, (blk, 2 * blk), 1)
        both_ok = (ki >= qi) & (ki <= qi + blk)
        first_ok = (lax.broadcasted_iota(jnp.int32, (blk, blk), 1)
                    <= lax.broadcasted_iota(jnp.int32, (blk, blk), 0))
        for r in range(dil):
            for n in range(nb):
                start = r + n * span
                q = q_ref[_rows(start, blk, dil)].astype(BF16)
                if n == 0:
                    kk = k_ref[_rows(start, blk, dil)].astype(BF16)
                    vv = v_ref[_rows(start, blk, dil)].astype(BF16)
                    ok = first_ok
                else:
                    kk = k_ref[_rows(start - span, 2 * blk, dil)].astype(BF16)
                    vv = v_ref[_rows(start - span, 2 * blk, dil)].astype(BF16)
                    ok = both_ok
                s = jnp.where(ok, _nt(q, kk) * scale, -jnp.inf)
                mx = jnp.max(s, axis=-1, keepdims=True)
                p = jnp.exp(s - mx)
                den = jnp.sum(p, axis=-1, keepdims=True)
                o = jnp.dot(p.astype(BF16), vv, preferred_element_type=F32) / den
                o_scr[g][_rows(start, blk, dil)] = o
                l_scr[g][_rows(start, blk, dil)] = jnp.broadcast_to(mx + jnp.log(den), (blk, HEAD_DIM))
    for c in range(seq // merge_rows):
        sl = slice(c * merge_rows, (c + 1) * merge_rows)
        ls = [l_scr[g][sl, :] for g in range(N_GROUPS)]
        mx = functools.reduce(jnp.maximum, ls)
        es = [jnp.exp(l - mx) for l in ls]
        den = functools.reduce(lambda a, b: a + b, es)
        acc = es[0] * o_scr[0][sl, :]
        for g in range(1, N_GROUPS):
            acc = acc + es[g] * o_scr[g][sl, :]
        out_ref[sl, :] = (acc / den).astype(out_ref.dtype)


def prompt_attention(qkvh, *, n_heads):
    B, _, S, _ = qkvh.shape
    for window, dil in DSA_GROUPS:
        assert S % window == 0
    in_specs = []
    for g in range(N_GROUPS):
        for t in range(3):
            in_specs.append(pl.BlockSpec((None, None, S, HEAD_DIM),
                                         lambda b, h, g=g, t=t: (b, (g * 3 + t) * n_heads + h, 0, 0)))
    kern = functools.partial(_prompt_attn_kernel, seq=S, merge_rows=min(256, S))
    return pl.pallas_call(
        kern,
        grid=(B, n_heads),
        in_specs=in_specs,
        out_specs=pl.BlockSpec((None, S, HEAD_DIM), lambda b, h: (b, 0, h)),
        out_shape=jax.ShapeDtypeStruct((B, S, n_heads * HEAD_DIM), BF16),
        scratch_shapes=[pltpu.VMEM((S, HEAD_DIM), F32)] * (2 * N_GROUPS),
        compiler_params=_params("parallel", "parallel"),
        name="prompt_attention",
    )(*([qkvh] * (3 * N_GROUPS)))


def _kv_export_kernel(*refs, n_layers, n_heads, ts):
    out_ref = refs[2 * n_layers]
    layer = pl.program_id(0)
    for l in range(n_layers):
        k_ref, v_ref = refs[2 * l], refs[2 * l + 1]

        @pl.when(layer == l)
        def _():
            for h in range(n_heads):
                out_ref[pl.ds(h, ts, stride=2 * n_heads), :] = k_ref[h]
                out_ref[pl.ds(n_heads + h, ts, stride=2 * n_heads), :] = v_ref[h]


def kv_export(qkvh_layers, group, *, keep, n_heads, ts):
    n_layers = len(qkvh_layers)
    B, _, S, _ = qkvh_layers[0].shape
    nt = keep // ts
    t0 = (S - keep) // ts
    in_specs, args = [], []
    for l, arr in enumerate(qkvh_layers):
        for t in (1, 2):
            def idx(ll, b, i, l=l, t=t):
                bb = jnp.where(ll == l, b, jnp.where(ll < l, 0, B - 1))
                ii = jnp.where(ll == l, i, jnp.where(ll < l, 0, nt - 1))
                return (bb, group * 3 + t, t0 + ii, 0)
            in_specs.append(pl.BlockSpec((None, n_heads, ts, HEAD_DIM), idx))
            args.append(arr)
    rows = 2 * n_heads
    out = pl.pallas_call(
        functools.partial(_kv_export_kernel, n_layers=n_layers, n_heads=n_heads, ts=ts),
        grid=(n_layers, B, nt),
        in_specs=in_specs,
        out_specs=pl.BlockSpec((None, None, ts * rows, HEAD_DIM), lambda ll, b, i: (ll, b, i, 0)),
        out_shape=jax.ShapeDtypeStruct((n_layers, B, keep * rows, HEAD_DIM), F32),
        compiler_params=_params("arbitrary", "arbitrary", "arbitrary"),
        name=f"kv_export_g{group}",
    )(*args)
    return out.reshape(n_layers, B, keep, 2, n_heads, HEAD_DIM)


def _sample_attn_kernel(qkv_ref, c0_ref, c1_ref, c2_ref, out_ref, *, n_heads):
    scale = HEAD_DIM ** -0.5
    os, ls = [], []
    for g, c_ref in enumerate((c0_ref, c1_ref, c2_ref)):
        q, k_new, v_new = qkv_ref[3 * g], qkv_ref[3 * g + 1], qkv_ref[3 * g + 2]
        kc = c_ref[:, 0:n_heads, :]
        vc = c_ref[:, n_heads:2 * n_heads, :]
        s_c = jnp.sum(kc * q[None], axis=-1, keepdims=True) * scale
        s_n = jnp.sum(k_new * q, axis=-1, keepdims=True) * scale
        mx = jnp.maximum(jnp.max(s_c, axis=0), s_n)
        p_c = jnp.exp(s_c - mx[None])
        p_n = jnp.exp(s_n - mx)
        den = jnp.sum(p_c, axis=0) + p_n
        os.append((jnp.sum(p_c * vc, axis=0) + p_n * v_new) / den)
        ls.append(mx + jnp.log(den))
    mx = functools.reduce(jnp.maximum, ls)
    es = [jnp.exp(l - mx) for l in ls]
    den = functools.reduce(lambda a, b: a + b, es)
    acc = es[0] * os[0]
    for g in range(1, N_GROUPS):
        acc = acc + es[g] * os[g]
    out_ref[...] = (acc / den).astype(out_ref.dtype)


def sample_attention(qkv, caches, layer, *, n_heads):
    B = qkv.shape[0]
    rows = 2 * n_heads
    views, specs = [], []
    for (window, dil), c in zip(DSA_GROUPS, caches):
        n_buf = c.shape[2]
        assert n_buf == window and n_buf % dil == 0
        views.append(c.reshape(c.shape[0], B, n_buf // dil, dil * rows, HEAD_DIM))
        specs.append(pl.BlockSpec((None, None, n_buf // dil, rows, HEAD_DIM), lambda b: (layer, b, 0, 0, 0)))
    return pl.pallas_call(
        functools.partial(_sample_attn_kernel, n_heads=n_heads),
        grid=(B,),
        in_specs=[pl.BlockSpec((None, 3 * N_GROUPS, n_heads, HEAD_DIM), lambda b: (b, 0, 0, 0))] + specs,
        out_specs=pl.BlockSpec((None, n_heads, HEAD_DIM), lambda b: (b, 0, 0)),
        out_shape=jax.ShapeDtypeStruct((B, n_heads, HEAD_DIM), F32),
        compiler_params=_params("parallel"),
        name="sample_attention",
    )(qkv, *views)


def _cache_shift_kernel(cur_ref, nxt_ref, new_ref, out_ref):
    k = pl.program_id(1)
    R = out_ref.shape[0]
    out_ref[0:R - 1] = cur_ref[1:R]
    last = k == pl.num_programs(1) - 1

    @pl.when(last)
    def _():
        out_ref[R - 1] = new_ref[0]

    @pl.when(jnp.logical_not(last))
    def _():
        out_ref[R - 1] = nxt_ref[0]


def cache_shift(cache, new, *, block):
    NL, B, n_buf = cache.shape[:3]
    rows = cache.shape[3] * cache.shape[4]
    c4 = cache.reshape(NL * B, n_buf, rows, HEAD_DIM)
    n4 = new.reshape(NL * B, 1, rows, HEAD_DIM)
    R = min(block, n_buf)
    out = pl.pallas_call(
        _cache_shift_kernel,
        grid=(NL * B, n_buf // R),
        in_specs=[pl.BlockSpec((None, R, rows, HEAD_DIM), lambda lb, k: (lb, k, 0, 0)),
                  pl.BlockSpec((None, 1, rows, HEAD_DIM), lambda lb, k: (lb, jnp.minimum((k + 1) * R, n_buf - 1), 0, 0)),
                  pl.BlockSpec((None, 1, rows, HEAD_DIM), lambda lb, k: (lb, 0, 0, 0))],
        out_specs=pl.BlockSpec((None, R, rows, HEAD_DIM), lambda lb, k: (lb, k, 0, 0)),
        out_shape=jax.ShapeDtypeStruct(c4.shape, c4.dtype),
        compiler_params=_params("parallel", "parallel"),
        name="cache_shift",
    )(c4, c4, n4)
    return out.reshape(cache.shape)


def kernel(x_prompt, x_sample, cache_kv_w128, cache_kv_w512, cache_kv_w2048, state_gla, state_ffn_conv,
           norm_mix, norm_ffn, dsa_w_qkv, dsa_q_gain, dsa_k_gain, dsa_w_o,
           gla_w_in, gla_w_gate2, gla_b_gate, gla_norm_gain, gla_w_o,
           ffn_w_in, ffn_conv_w, ffn_conv_b, ffn_w_out):
    B, S, D = x_prompt.shape
    BS = x_sample.shape[0]
    assert x_sample.shape[1] == 1
    depth = norm_mix.shape[0]
    caches = (cache_kv_w128, cache_kv_w512, cache_kv_w2048)
    n_gla, gla_heads, gla_dk, gla_dv = state_gla.shape[0], state_gla.shape[2], state_gla.shape[3], state_gla.shape[4]
    gla_kd, gla_vd = gla_heads * gla_dk, gla_heads * gla_dv
    dsa_heads = dsa_w_o.shape[1] // HEAD_DIM
    W = dsa_heads * HEAD_DIM
    d_ff = ffn_w_in.shape[2] // 2
    M = B * S
    R = SAMPLE_ROWS

    tm = min(1024, S)
    tn = min(512, W)
    tw = 256
    gla_chunk = min(256, S)
    gla_hps = min(4, gla_heads)

    xp = x_prompt.reshape(M, D)
    xs = jnp.pad(x_sample.reshape(BS, D), ((0, R - BS), (0, 0)))

    norm_mix3 = norm_mix.reshape(depth, 1, D)
    norm_ffn3 = norm_ffn.reshape(depth, 1, D)
    conv_b3 = ffn_conv_b.reshape(depth, 1, d_ff)
    n_main = 2 * gla_kd + 2 * gla_vd
    w_gl = jnp.pad(gla_w_in[:, :, n_main:], ((0, 0), (0, 0), (0, LANES - GLA_GATE_RANK)))
    gla_w_in_t = jnp.swapaxes(gla_w_in, 1, 2)
    w_gate2 = jnp.pad(gla_w_gate2, ((0, 0), (0, LANES - GLA_GATE_RANK), (0, 0)))
    b_gate3 = gla_b_gate.reshape(n_gla, 1, gla_kd)
    gla_ng3 = gla_norm_gain.reshape(n_gla, 1, gla_dv)
    ones = jnp.ones_like(dsa_q_gain)
    qk_gains = jnp.stack([dsa_q_gain, dsa_k_gain, ones], axis=2).reshape(-1, 1, HEAD_DIM)
    rope_p = _rope_tables(jnp.arange(S))
    rope_s = _rope_tables(jnp.full((R,), PAST_LEN, jnp.int32))

    qkvh_layers = []
    kv_new = [[] for _ in DSA_GROUPS]
    gla_p, gla_s, conv_p, conv_s = [], [], [], []
    zero_state = jnp.zeros((1, B, gla_heads, gla_dk, gla_dv), F32)

    for i in range(depth):
        j = i // 2
        hp = rms_bf16(xp, norm_mix3, i, 256)
        hs = rms_bf16(xs, norm_mix3, i, R)
        if i % 2 == 0:
            h, h_s = mm_wres(hp, hs, gla_w_in_t, j, tm=tm, tn=tn, ncols=n_main, w_transposed=True, name="gla_in")
            gl, gl_s = mm_wres(hp, hs, w_gl, j, tm=tm, tn=LANES, name="gla_gate_in")
            o, sp = gla_core(h.reshape(B, S, n_main), gl.reshape(B, S, LANES), w_gate2, b_gate3, gla_ng3,
                             zero_state, j, 0, seq_len=S, C=gla_chunk, n_heads=gla_heads, dk=gla_dk, dv=gla_dv,
                             heads_per_step=gla_hps)
            h_s = jnp.pad(h_s[:BS].reshape(BS, 1, n_main), ((0, 0), (0, R - 1), (0, 0)))
            gl_s = jnp.pad(gl_s[:BS].reshape(BS, 1, LANES), ((0, 0), (0, R - 1), (0, 0)))
            o_s, ss = gla_core(h_s, gl_s, w_gate2, b_gate3, gla_ng3, state_gla, j, j,
                               seq_len=1, C=R, n_heads=gla_heads, dk=gla_dk, dv=gla_dv, heads_per_step=gla_hps)
            o_s = jnp.pad(o_s[:, 0], ((0, R - BS), (0, 0)))
            xp, xs = mm_wres(o.reshape(M, gla_vd), o_s, gla_w_o, j, tm=tm, tn=tn, residual=xp, residual_s=xs,
                             name="gla_out")
            gla_p.append(sp)
            gla_s.append(ss)
        else:
            qkvh, qkv_s = dsa_qkv(hp, hs, dsa_w_qkv, qk_gains, rope_p, rope_s, j, tm=tm, tn=tn, section=W, seq=S)
            merged = prompt_attention(qkvh, n_heads=dsa_heads)
            qkvh_layers.append(qkvh)
            qkv_s = qkv_s[:BS].reshape(BS, N_GROUPS, 3, dsa_heads, HEAD_DIM)
            merged_s = sample_attention(qkv_s.reshape(BS, 3 * N_GROUPS, dsa_heads, HEAD_DIM), caches, j,
                                        n_heads=dsa_heads)
            merged_s = jnp.pad(merged_s.reshape(BS, W).astype(BF16), ((0, R - BS), (0, 0)))
            xp, xs = mm_wres(merged.reshape(M, W), merged_s, dsa_w_o, j, tm=tm, tn=tn, residual=xp,
                             residual_s=xs, name="dsa_out")
            for g in range(N_GROUPS):
                kv_new[g].append(qkv_s[:, g, 1:3])
        fp = rms_bf16(xp, norm_ffn3, i, 256)
        fs = rms_bf16(xs, norm_ffn3, i, R)
        st = jnp.pad(state_ffn_conv[i], ((0, R - BS), (0, 0), (0, 0)))
        act, cp, act_s, g_s = ffn_in(fp, fs, ffn_w_in, ffn_conv_w, conv_b3, st[:, 0], st[:, 1], i,
                                     seq=S, tm=tm, tw=tw)
        xp, xs = mm_fullk(act, act_s, ffn_w_out, i, xp, xs, tm=tm, tn=tw, name="ffn_out")
        conv_p.append(cp)
        conv_s.append(jnp.stack([state_ffn_conv[i][:, 1], g_s[:BS]], axis=1))

    kv128_p, kv512_p, kv2048_p = [
        kv_export(qkvh_layers, g, keep=min(window, S), n_heads=dsa_heads, ts=min(128, S))
        for g, (window, _) in enumerate(DSA_GROUPS)]
    kv128_s, kv512_s, kv2048_s = [cache_shift(c, jnp.stack(n, axis=0), block=256)
                                  for c, n in zip(caches, kv_new)]
    return (xp.reshape(B, S, D), xs[:BS].reshape(BS, 1, D),
            kv128_p, kv128_s, kv512_p, kv512_s, kv2048_p, kv2048_s,
            jnp.stack(gla_p, axis=0), jnp.stack(gla_s, axis=0),
            jnp.stack(conv_p, axis=0), jnp.stack(conv_s, axis=0))
```

```python
import functools
import math

import numpy as np
import jax
import jax.numpy as jnp
from jax import lax
from jax.experimental import pallas as pl
from jax.experimental.pallas import tpu as pltpu

F32 = jnp.float32
BF16 = jnp.bfloat16

EPS = 1e-6
HEAD_DIM = 128
ROT_DIM = HEAD_DIM // 4
ROPE_THETA = 500000.0
DSA_GROUPS = ((128, 1), (512, 4), (2048, 16))
N_GROUPS = len(DSA_GROUPS)
GLA_GATE_RANK = 16
GLA_GATE_TAU = 16.0
CONV_W = 3
PAST_LEN = 8192
LANES = 128
SUBLANES = 8
SAMPLE_ROWS = 16
VMEM_LIMIT_BYTES = 56 * 1024 * 1024


def _params(*sem):
    return pltpu.CompilerParams(dimension_semantics=sem, vmem_limit_bytes=VMEM_LIMIT_BYTES)


def _rms_kernel(x_ref, g_ref, o_ref):
    x = x_ref[...]
    y = x * lax.rsqrt(jnp.mean(x * x, axis=-1, keepdims=True) + EPS)
    o_ref[...] = (y * g_ref[...]).astype(o_ref.dtype)


def rms_bf16(x, gains, layer, tm):
    M, D = x.shape
    return pl.pallas_call(
        _rms_kernel,
        grid=(M // tm,),
        in_specs=[pl.BlockSpec((tm, D), lambda i: (i, 0)),
                  pl.BlockSpec((None, 1, D), lambda i: (layer, 0, 0))],
        out_specs=pl.BlockSpec((tm, D), lambda i: (i, 0)),
        out_shape=jax.ShapeDtypeStruct((M, D), BF16),
        compiler_params=_params("parallel"),
        name="rms_bf16",
    )(x, gains)


def _mm_wres_kernel(*refs, n_extra, n_extra_s, has_res, epilogue, epilogue_s, w_transposed):
    it = iter(refs)
    a_ref, w_ref = next(it), next(it)
    extra = [next(it) for _ in range(n_extra)]
    r_ref = next(it) if has_res else None
    as_ref = next(it)
    extra_s = [next(it) for _ in range(n_extra_s)]
    rs_ref = next(it) if has_res else None
    o_ref, os_ref, wb_ref = next(it), next(it), next(it)
    i = pl.program_id(1)

    def product_of(lhs_ref, res_ref):
        def product(rows=slice(None)):
            if w_transposed:
                acc = _nt(lhs_ref[rows, :], wb_ref[...])
            else:
                acc = jnp.dot(lhs_ref[rows, :], wb_ref[...], preferred_element_type=F32)
            if has_res:
                acc = acc + res_ref[rows, :]
            return acc
        return product

    @pl.when(i == 0)
    def _():
        wb_ref[...] = w_ref[...].astype(BF16)
        if epilogue_s is None:
            os_ref[...] = product_of(as_ref, rs_ref)().astype(os_ref.dtype)
        else:
            epilogue_s(product_of(as_ref, rs_ref), os_ref, *extra_s)

    if epilogue is None:
        o_ref[...] = product_of(a_ref, r_ref)().astype(o_ref.dtype)
    else:
        epilogue(product_of(a_ref, r_ref), o_ref, *extra)


def mm_wres(a, a_s, w, layer, *, tm, tn, col0=0, ncols=None, out_dtype=F32, residual=None, residual_s=None,
            epilogue=None, epilogue_s=None, extra=(), extra_specs=(), extra_s=(), extra_specs_s=(),
            out_spec=None, out_shape=None, w_transposed=False, name="mm_wres"):
    M, K = a.shape
    R = a_s.shape[0]
    ncols = w.shape[1 if w_transposed else 2] - col0 if ncols is None else ncols
    cb0 = col0 // tn
    if w_transposed:
        w_spec = pl.BlockSpec((None, tn, K), lambda j, i: (layer, cb0 + j, 0))
    else:
        w_spec = pl.BlockSpec((None, K, tn), lambda j, i: (layer, 0, cb0 + j))
    if out_spec is None:
        out_spec = pl.BlockSpec((tm, tn), lambda j, i: (i, j))
        out_shape = jax.ShapeDtypeStruct((M, ncols), out_dtype)
    has_res = residual is not None
    in_specs = [pl.BlockSpec((tm, K), lambda j, i: (i, 0)), w_spec]
    in_specs += list(extra_specs)
    args = [a, w, *extra]
    if has_res:
        in_specs.append(pl.BlockSpec((tm, tn), lambda j, i: (i, j)))
        args.append(residual)
    in_specs.append(pl.BlockSpec((R, K), lambda j, i: (0, 0)))
    in_specs += list(extra_specs_s)
    args += [a_s, *extra_s]
    if has_res:
        in_specs.append(pl.BlockSpec((R, tn), lambda j, i: (0, j)))
        args.append(residual_s)
    kern = functools.partial(_mm_wres_kernel, n_extra=len(extra), n_extra_s=len(extra_s), has_res=has_res,
                             epilogue=epilogue, epilogue_s=epilogue_s, w_transposed=w_transposed)
    return pl.pallas_call(
        kern,
        grid=(ncols // tn, M // tm),
        in_specs=in_specs,
        out_specs=[out_spec, pl.BlockSpec((R, tn), lambda j, i: (0, j))],
        out_shape=[out_shape, jax.ShapeDtypeStruct((R, ncols), out_dtype)],
        scratch_shapes=[pltpu.VMEM((tn, K) if w_transposed else (K, tn), BF16)],
        compiler_params=_params("parallel", "arbitrary"),
        name=name,
    )(*args)


def _mm_fullk_kernel(a_ref, w_ref, r_ref, as_ref, rs_ref, o_ref, os_ref):
    wb = w_ref[...].astype(BF16)
    o_ref[...] = r_ref[...] + jnp.dot(a_ref[...], wb, preferred_element_type=F32)

    @pl.when(pl.program_id(0) == 0)
    def _():
        os_ref[...] = rs_ref[...] + jnp.dot(as_ref[...], wb, preferred_element_type=F32)

    @pl.when(pl.program_id(0) > 0)
    def _():
        os_ref[...] = jnp.zeros_like(os_ref)


def mm_fullk(a, a_s, w, layer, residual, residual_s, *, tm, tn, name="mm_fullk"):
    M, K = a.shape
    R = a_s.shape[0]
    N = w.shape[2]
    out, out_s = pl.pallas_call(
        _mm_fullk_kernel,
        grid=(M // tm, N // tn),
        in_specs=[pl.BlockSpec((tm, K), lambda i, j: (i, 0), pipeline_mode=pl.Buffered(1)),
                  pl.BlockSpec((None, K, tn), lambda i, j: (layer, 0, j)),
                  pl.BlockSpec((tm, tn), lambda i, j: (i, j)),
                  pl.BlockSpec((R, K), lambda i, j: (0, 0)),
                  pl.BlockSpec((R, tn), lambda i, j: (0, j))],
        out_specs=[pl.BlockSpec((tm, tn), lambda i, j: (i, j)),
                   pl.BlockSpec((None, R, tn), lambda i, j: (i, 0, j))],
        out_shape=[jax.ShapeDtypeStruct((M, N), F32), jax.ShapeDtypeStruct((M // tm, R, N), F32)],
        compiler_params=_params("arbitrary", "arbitrary"),
        name=name,
    )(a, w, residual, a_s, residual_s)
    return out, out_s[0]


def _silu(x):
    return x * jax.nn.sigmoid(x)


def _ffn_in_kernel(a_ref, wg_ref, wu_ref, cw_ref, cb_ref, as_ref, s0_ref, s1_ref,
                   act_ref, st_ref, acts_ref, gs_ref, wgb_ref, wub_ref, carry_ref, *, tiles_per_seq):
    i = pl.program_id(1)
    cw = cw_ref[...]
    cb = cb_ref[...]

    @pl.when(i == 0)
    def _():
        wgb_ref[...] = wg_ref[...].astype(BF16)
        wub_ref[...] = wu_ref[...].astype(BF16)
        a_s = as_ref[...]
        g_s = jnp.dot(a_s, wgb_ref[...], preferred_element_type=F32)
        u_s = jnp.dot(a_s, wub_ref[...], preferred_element_type=F32)
        c_s = cb + cw[0:1] * s0_ref[...] + cw[1:2] * s1_ref[...] + cw[2:3] * g_s
        acts_ref[...] = (_silu(c_s) * u_s).astype(acts_ref.dtype)
        gs_ref[...] = g_s

    @pl.when(i % tiles_per_seq == 0)
    def _():
        carry_ref[...] = jnp.zeros_like(carry_ref)

    a = a_ref[...]
    g = jnp.dot(a, wgb_ref[...], preferred_element_type=F32)
    u = jnp.dot(a, wub_ref[...], preferred_element_type=F32)
    tm = g.shape[0]
    prev = carry_ref[...]
    row = lax.broadcasted_iota(jnp.int32, g.shape, 0)
    g1 = jnp.where(row == 0, prev[7:8], pltpu.roll(g, 1, axis=0))
    g2 = jnp.where(row == 0, prev[6:7],
                   jnp.where(row == 1, prev[7:8], pltpu.roll(g, 2, axis=0)))
    c = cb + cw[0:1] * g2 + cw[1:2] * g1 + cw[2:3] * g
    act_ref[...] = (_silu(c) * u).astype(act_ref.dtype)
    carry_ref[...] = g[tm - 8:tm]
    st_ref[...] = g[tm - (CONV_W - 1):tm]


def ffn_in(a, a_s, w_in, conv_w, conv_b, st0, st1, layer, *, seq, tm, tw):
    M, K = a.shape
    R = a_s.shape[0]
    d_ff = w_in.shape[2] // 2
    nb = M // seq
    tps = seq // tm
    ub0 = d_ff // tw
    kern = functools.partial(_ffn_in_kernel, tiles_per_seq=tps)
    return pl.pallas_call(
        kern,
        grid=(d_ff // tw, M // tm),
        in_specs=[pl.BlockSpec((tm, K), lambda j, i: (i, 0)),
                  pl.BlockSpec((None, K, tw), lambda j, i: (layer, 0, j)),
                  pl.BlockSpec((None, K, tw), lambda j, i: (layer, 0, ub0 + j)),
                  pl.BlockSpec((None, CONV_W, tw), lambda j, i: (layer, 0, j)),
                  pl.BlockSpec((None, 1, tw), lambda j, i: (layer, 0, j)),
                  pl.BlockSpec((R, K), lambda j, i: (0, 0)),
                  pl.BlockSpec((R, tw), lambda j, i: (0, j)),
                  pl.BlockSpec((R, tw), lambda j, i: (0, j))],
        out_specs=[pl.BlockSpec((tm, tw), lambda j, i: (i, j)),
                   pl.BlockSpec((None, CONV_W - 1, tw), lambda j, i: (i // tps, 0, j)),
                   pl.BlockSpec((R, tw), lambda j, i: (0, j)),
                   pl.BlockSpec((R, tw), lambda j, i: (0, j))],
        out_shape=[jax.ShapeDtypeStruct((M, d_ff), BF16),
                   jax.ShapeDtypeStruct((nb, CONV_W - 1, d_ff), F32),
                   jax.ShapeDtypeStruct((R, d_ff), BF16),
                   jax.ShapeDtypeStruct((R, d_ff), F32)],
        scratch_shapes=[pltpu.VMEM((K, tw), BF16), pltpu.VMEM((K, tw), BF16),
                        pltpu.VMEM((8, tw), F32)],
        compiler_params=_params("parallel", "arbitrary"),
        name="ffn_in",
    )(a, w_in, w_in, conv_w, conv_b, a_s, st0, st1)


def _gla_tables(C):
    n_levels = int(math.log2(C))
    assert 1 << n_levels == C
    i = np.arange(C)[:, None]
    t = np.arange(C)[None, :]
    mats = [t <= i]
    level = np.full((C, C), -1, np.int32)
    level[np.arange(C), np.arange(C)] = n_levels
    for lv in range(n_levels):
        s = 1 << lv
        mid = (i // (2 * s)) * (2 * s) + s
        upper = i >= mid
        if 2 * s <= SUBLANES:
            mats.append((upper & (t >= mid) & (t <= i)) | (~upper & (t > i) & (t < mid)))
        same = (i // (2 * s)) == (t // (2 * s))
        own = same & upper & ~((t % (2 * s)) >= s)
        level[own] = lv
    return np.concatenate(mats, axis=0).astype(np.float32), level, n_levels


def _nt(a, b):
    return lax.dot_general(a, b, (((1,), (1,)), ((), ())), preferred_element_type=F32)


def _tn(a, b):
    return lax.dot_general(a, b, (((0,), (0,)), ((), ())), preferred_element_type=F32)


def _gla_kernel(q_ref, k_ref, v_ref, r_ref, gl_ref, w2_ref, bg_ref, ng_ref, s0_ref, ms_ref, lvl_ref,
                o_ref, sout_ref, st_ref, *, C, n_levels, seq_len, dk, dv, heads):
    c = pl.program_id(2)
    nc = pl.num_programs(2)

    @pl.when(c == 0)
    def _():
        for hh in range(heads):
            st_ref[hh] = s0_ref[hh].T

    glb = gl_ref[...].astype(BF16)
    ms = ms_ref[...]
    lvl = lvl_ref[...]
    for hh in range(heads):
        ks = slice(hh * dk, (hh + 1) * dk)
        vs = slice(hh * dv, (hh + 1) * dv)
        z = jnp.dot(glb, w2_ref[:, ks].astype(BF16), preferred_element_type=F32) + bg_ref[:, ks]
        g = -(jnp.maximum(-z, 0.0) + jnp.log1p(jnp.exp(-jnp.abs(z)))) / GLA_GATE_TAU
        if seq_len % C:
            row = c * C + lax.broadcasted_iota(jnp.int32, g.shape, 0)
            g = jnp.where(row < seq_len, g, 0.0)
        g_hi = g.astype(BF16)
        g_lo = (g - g_hi.astype(F32)).astype(BF16)
        ex = jnp.dot(ms, g_hi, preferred_element_type=F32) + jnp.dot(ms, g_lo, preferred_element_type=F32)
        b = ex[0:C]
        e_b = jnp.exp(b)
        e_end = jnp.exp(b[C - 1:C] - b)

        q = q_ref[:, ks] * (dk ** -0.5)
        k = k_ref[:, ks]
        vb = v_ref[:, vs].astype(BF16)
        st = st_ref[hh]
        o = _nt((q * e_b).astype(BF16), st.astype(BF16))
        a = jnp.where(lvl == n_levels, _nt(q.astype(BF16), k.astype(BF16)), 0.0)
        for lv in range(n_levels):
            s = 1 << lv
            if 2 * s <= SUBLANES:
                e_l = jnp.exp(ex[(1 + lv) * C:(2 + lv) * C])
            else:
                m = jnp.broadcast_to(b.reshape(C // (2 * s), 2 * s, dk)[:, s - 1:s, :], (C // (2 * s), 2 * s, dk))
                e_l = jnp.exp(-jnp.abs(b - m.reshape(C, dk)))
            a = jnp.where(lvl == lv, _nt((q * e_l).astype(BF16), (k * e_l).astype(BF16)), a)
        o = o + jnp.dot(a.astype(BF16), vb, preferred_element_type=F32)
        decay_end = e_b[C - 1:C]
        st_ref[hh] = st * decay_end + _tn(vb, (k * e_end).astype(BF16))

        y = o * lax.rsqrt(jnp.mean(o * o, axis=-1, keepdims=True) + EPS) * ng_ref[...]
        o_ref[:, vs] = (y * _silu(r_ref[:, vs])).astype(o_ref.dtype)

    @pl.when(c == nc - 1)
    def _():
        for hh in range(heads):
            sout_ref[hh] = st_ref[hh].T


def gla_core(h, gl, w_gate2, b_gate, norm_gain, s0, layer, s0_layer, *, seq_len, C, n_heads, dk, dv,
             heads_per_step):
    B, Lp, _ = h.shape
    ms_np, lvl_np, n_levels = _gla_tables(C)
    ms = jnp.asarray(ms_np, BF16)
    lvl = jnp.asarray(lvl_np)
    hps = heads_per_step
    assert n_heads % hps == 0
    ng = n_heads // hps
    v0 = 2 * n_heads * dk // (hps * dv)
    r0 = v0 + ng
    kern = functools.partial(_gla_kernel, C=C, n_levels=n_levels, seq_len=seq_len, dk=dk, dv=dv, heads=hps)
    R = ms_np.shape[0]
    return pl.pallas_call(
        kern,
        grid=(B, ng, Lp // C),
        in_specs=[pl.BlockSpec((None, C, hps * dk), lambda b, hd, c: (b, c, hd)),
                  pl.BlockSpec((None, C, hps * dk), lambda b, hd, c: (b, c, ng + hd)),
                  pl.BlockSpec((None, C, hps * dv), lambda b, hd, c: (b, c, v0 + hd)),
                  pl.BlockSpec((None, C, hps * dv), lambda b, hd, c: (b, c, r0 + hd)),
                  pl.BlockSpec((None, C, LANES), lambda b, hd, c: (b, c, 0)),
                  pl.BlockSpec((None, LANES, hps * dk), lambda b, hd, c: (layer, 0, hd)),
                  pl.BlockSpec((None, 1, hps * dk), lambda b, hd, c: (layer, 0, hd)),
                  pl.BlockSpec((None, 1, dv), lambda b, hd, c: (layer, 0, 0)),
                  pl.BlockSpec((None, None, hps, dk, dv), lambda b, hd, c: (s0_layer, b, hd, 0, 0)),
                  pl.BlockSpec((R, C), lambda b, hd, c: (0, 0)),
                  pl.BlockSpec((C, C), lambda b, hd, c: (0, 0))],
        out_specs=[pl.BlockSpec((None, C, hps * dv), lambda b, hd, c: (b, c, hd)),
                   pl.BlockSpec((None, hps, dk, dv), lambda b, hd, c: (b, hd, 0, 0))],
        out_shape=[jax.ShapeDtypeStruct((B, Lp, n_heads * dv), BF16),
                   jax.ShapeDtypeStruct((B, n_heads, dk, dv), F32)],
        scratch_shapes=[pltpu.VMEM((hps, dv, dk), F32)],
        compiler_params=_params("parallel", "parallel", "arbitrary"),
        name="gla_core",
    )(h, h, h, h, gl, w_gate2, b_gate, norm_gain, s0, ms, lvl)


def _rope_tables(pos):
    half = ROT_DIM // 2
    inv = jnp.power(ROPE_THETA, -jnp.arange(half, dtype=F32) * (2.0 / ROT_DIM))
    ang = pos.astype(F32)[:, None] * inv[None, :]
    cos, sin = jnp.cos(ang), jnp.sin(ang)
    n = pos.shape[0]
    cos_t = jnp.concatenate([cos, cos, jnp.ones((n, HEAD_DIM - ROT_DIM), F32)], axis=1)
    sin_t = jnp.concatenate([-sin, sin, jnp.zeros((n, HEAD_DIM - ROT_DIM), F32)], axis=1)
    return cos_t, sin_t


def _qkv_epilogue(product, o_ref, gain_ref, cos_ref, sin_ref, *, tn, section, head_major, row_chunk):
    j = pl.program_id(0)
    kind = (j * tn // section) % 3
    tm = cos_ref.shape[0]
    chunks = [slice(r, r + row_chunk) for r in range(0, tm, row_chunk)]

    def put(rows, hh, val):
        if head_major:
            o_ref[hh, rows, :] = val
        else:
            o_ref[rows, hh * HEAD_DIM:(hh + 1) * HEAD_DIM] = val

    @pl.when(kind == 2)
    def _():
        for rows in chunks:
            acc = product(rows)
            for hh in range(tn // HEAD_DIM):
                put(rows, hh, acc[:, hh * HEAD_DIM:(hh + 1) * HEAD_DIM])

    @pl.when(kind != 2)
    def _():
        half = ROT_DIM // 2
        gain = gain_ref[...]
        lane = lax.broadcasted_iota(jnp.int32, (row_chunk, HEAD_DIM), 1)
        for rows in chunks:
            acc = product(rows)
            cos_t = cos_ref[rows, :]
            sin_t = sin_ref[rows, :]
            for hh in range(tn // HEAD_DIM):
                x = acc[:, hh * HEAD_DIM:(hh + 1) * HEAD_DIM]
                y = x * lax.rsqrt(jnp.mean(x * x, axis=-1, keepdims=True) + EPS) * gain
                swapped = jnp.where(lane < half, pltpu.roll(y, HEAD_DIM - half, axis=1),
                                    pltpu.roll(y, half, axis=1))
                put(rows, hh, y * cos_t + swapped * sin_t)


def dsa_qkv(a, a_s, w_qkv, gains, rope, rope_s, layer, *, tm, tn, section, seq):
    M = a.shape[0]
    R = a_s.shape[0]
    tps = seq // tm
    n_sec = w_qkv.shape[2] // section
    gain_spec = pl.BlockSpec((None, 1, HEAD_DIM), lambda j, i: (layer * n_sec + j * tn // section, 0, 0))
    extra_specs = [gain_spec,
                   pl.BlockSpec((tm, HEAD_DIM), lambda j, i: (i % tps, 0)),
                   pl.BlockSpec((tm, HEAD_DIM), lambda j, i: (i % tps, 0))]
    extra_specs_s = [gain_spec,
                     pl.BlockSpec((R, HEAD_DIM), lambda j, i: (0, 0)),
                     pl.BlockSpec((R, HEAD_DIM), lambda j, i: (0, 0))]
    epi = functools.partial(_qkv_epilogue, tn=tn, section=section, head_major=True, row_chunk=min(256, tm))
    epi_s = functools.partial(_qkv_epilogue, tn=tn, section=section, head_major=False, row_chunk=R)
    hpb = tn // HEAD_DIM
    out_spec = pl.BlockSpec((None, hpb, tm, HEAD_DIM), lambda j, i: (i // tps, j, i % tps, 0))
    out_shape = jax.ShapeDtypeStruct((M // seq, w_qkv.shape[2] // HEAD_DIM, seq, HEAD_DIM), F32)
    return mm_wres(a, a_s, w_qkv, layer, tm=tm, tn=tn, epilogue=epi, epilogue_s=epi_s,
                   extra=(gains, *rope), extra_specs=extra_specs,
                   extra_s=(gains, *rope_s), extra_specs_s=extra_specs_s,
                   out_spec=out_spec, out_shape=out_shape, name="dsa_qkv")


def _rows(start, size, stride):
    if stride == 1:
        return (pl.ds(start, size), slice(None))
    return (pl.ds(start, size, stride=stride), slice(None))


def _prompt_attn_kernel(*refs, seq, merge_rows):
    qkv_refs = refs[:3 * N_GROUPS]
    out_ref = refs[3 * N_GROUPS]
    o_scr = refs[3 * N_GROUPS + 1:3 * N_GROUPS + 1 + N_GROUPS]
    l_scr = refs[3 * N_GROUPS + 1 + N_GROUPS:]
    scale = HEAD_DIM ** -0.5
    for g, (window, dil) in enumerate(DSA_GROUPS):
        q_ref, k_ref, v_ref = qkv_refs[3 * g:3 * g + 3]
        blk = window // dil
        span = blk * dil
        nb = seq // span
        qi = lax.broadcasted_iota(jnp.int32, (blk, 2 * blk), 0)
        ki = lax.broadcasted_iota(jnp.int32, (blk, 2 * blk), 1)
        both_ok = (ki >= qi) & (ki <= qi + blk)
        first_ok = (lax.broadcasted_iota(jnp.int32, (blk, blk), 1)
                    <= lax.broadcasted_iota(jnp.int32, (blk, blk), 0))
        for r in range(dil):
            for n in range(nb):
                start = r + n * span
                q = q_ref[_rows(start, blk, dil)].astype(BF16)
                if n == 0:
                    kk = k_ref[_rows(start, blk, dil)].astype(BF16)
                    vv = v_ref[_rows(start, blk, dil)].astype(BF16)
                    ok = first_ok
                else:
                    kk = k_ref[_rows(start - span, 2 * blk, dil)].astype(BF16)
                    vv = v_ref[_rows(start - span, 2 * blk, dil)].astype(BF16)
                    ok = both_ok
                s = jnp.where(ok, _nt(q, kk) * scale, -jnp.inf)
                mx = jnp.max(s, axis=-1, keepdims=True)
                p = jnp.exp(s - mx)
                den = jnp.sum(p, axis=-1, keepdims=True)
                o = jnp.dot(p.astype(BF16), vv, preferred_element_type=F32) / den
                o_scr[g][_rows(start, blk, dil)] = o
                l_scr[g][_rows(start, blk, dil)] = jnp.broadcast_to(mx + jnp.log(den), (blk, HEAD_DIM))
    for c in range(seq // merge_rows):
        sl = slice(c * merge_rows, (c + 1) * merge_rows)
        ls = [l_scr[g][sl, :] for g in range(N_GROUPS)]
        mx = functools.reduce(jnp.maximum, ls)
        es = [jnp.exp(l - mx) for l in ls]
        den = functools.reduce(lambda a, b: a + b, es)
        acc = es[0] * o_scr[0][sl, :]
        for g in range(1, N_GROUPS):
            acc = acc + es[g] * o_scr[g][sl, :]
        out_ref[sl, :] = (acc / den).astype(out_ref.dtype)


def prompt_attention(qkvh, *, n_heads):
    B, _, S, _ = qkvh.shape
    for window, dil in DSA_GROUPS:
        assert S % window == 0
    in_specs = []
    for g in range(N_GROUPS):
        for t in range(3):
            in_specs.append(pl.BlockSpec((None, None, S, HEAD_DIM),
                                         lambda b, h, g=g, t=t: (b, (g * 3 + t) * n_heads + h, 0, 0)))
    kern = functools.partial(_prompt_attn_kernel, seq=S, merge_rows=min(256, S))
    return pl.pallas_call(
        kern,
        grid=(B, n_heads),
        in_specs=in_specs,
        out_specs=pl.BlockSpec((None, S, HEAD_DIM), lambda b, h: (b, 0, h)),
        out_shape=jax.ShapeDtypeStruct((B, S, n_heads * HEAD_DIM), BF16),
        scratch_shapes=[pltpu.VMEM((S, HEAD_DIM), F32)] * (2 * N_GROUPS),
        compiler_params=_params("parallel", "parallel"),
        name="prompt_attention",
    )(*([qkvh] * (3 * N_GROUPS)))


def _kv_export_kernel(*refs, n_layers, n_heads, ts):
    out_ref = refs[2 * n_layers]
    layer = pl.program_id(0)
    for l in range(n_layers):
        k_ref, v_ref = refs[2 * l], refs[2 * l + 1]

        @pl.when(layer == l)
        def _():
            kt = jnp.swapaxes(k_ref[...], 0, 1)
            vt = jnp.swapaxes(v_ref[...], 0, 1)
            out_ref[...] = jnp.concatenate([kt, vt], axis=1).reshape(ts * 2 * n_heads, HEAD_DIM)


def kv_export(qkvh_layers, group, *, keep, n_heads, ts):
    n_layers = len(qkvh_layers)
    B, _, S, _ = qkvh_layers[0].shape
    nt = keep // ts
    t0 = (S - keep) // ts
    in_specs, args = [], []
    for l, arr in enumerate(qkvh_layers):
        for t in (1, 2):
            def idx(ll, b, i, l=l, t=t):
                bb = jnp.where(ll == l, b, jnp.where(ll < l, 0, B - 1))
                ii = jnp.where(ll == l, i, jnp.where(ll < l, 0, nt - 1))
                return (bb, group * 3 + t, t0 + ii, 0)
            in_specs.append(pl.BlockSpec((None, n_heads, ts, HEAD_DIM), idx))
            args.append(arr)
    rows = 2 * n_heads
    out = pl.pallas_call(
        functools.partial(_kv_export_kernel, n_layers=n_layers, n_heads=n_heads, ts=ts),
        grid=(n_layers, B, nt),
        in_specs=in_specs,
        out_specs=pl.BlockSpec((None, None, ts * rows, HEAD_DIM), lambda ll, b, i: (ll, b, i, 0)),
        out_shape=jax.ShapeDtypeStruct((n_layers, B, keep * rows, HEAD_DIM), F32),
        compiler_params=_params("arbitrary", "arbitrary", "arbitrary"),
        name=f"kv_export_g{group}",
    )(*args)
    return out.reshape(n_layers, B, keep, 2, n_heads, HEAD_DIM)


def _sample_attn_kernel(qkv_ref, c0_ref, c1_ref, c2_ref, out_ref, *, n_heads):
    scale = HEAD_DIM ** -0.5
    os, ls = [], []
    for g, c_ref in enumerate((c0_ref, c1_ref, c2_ref)):
        q, k_new, v_new = qkv_ref[3 * g], qkv_ref[3 * g + 1], qkv_ref[3 * g + 2]
        kc = c_ref[:, 0:n_heads, :]
        vc = c_ref[:, n_heads:2 * n_heads, :]
        s_c = jnp.sum(kc * q[None], axis=-1, keepdims=True) * scale
        s_n = jnp.sum(k_new * q, axis=-1, keepdims=True) * scale
        mx = jnp.maximum(jnp.max(s_c, axis=0), s_n)
        p_c = jnp.exp(s_c - mx[None])
        p_n = jnp.exp(s_n - mx)
        den = jnp.sum(p_c, axis=0) + p_n
        os.append((jnp.sum(p_c * vc, axis=0) + p_n * v_new) / den)
        ls.append(mx + jnp.log(den))
    mx = functools.reduce(jnp.maximum, ls)
    es = [jnp.exp(l - mx) for l in ls]
    den = functools.reduce(lambda a, b: a + b, es)
    acc = es[0] * os[0]
    for g in range(1, N_GROUPS):
        acc = acc + es[g] * os[g]
    out_ref[...] = (acc / den).astype(out_ref.dtype)


def sample_attention(qkv, caches, layer, *, n_heads):
    B = qkv.shape[0]
    rows = 2 * n_heads
    views, specs = [], []
    for (window, dil), c in zip(DSA_GROUPS, caches):
        n_buf = c.shape[2]
        assert n_buf == window and n_buf % dil == 0
        views.append(c.reshape(c.shape[0], B, n_buf // dil, dil * rows, HEAD_DIM))
        specs.append(pl.BlockSpec((None, None, n_buf // dil, rows, HEAD_DIM), lambda b: (layer, b, 0, 0, 0)))
    return pl.pallas_call(
        functools.partial(_sample_attn_kernel, n_heads=n_heads),
        grid=(B,),
        in_specs=[pl.BlockSpec((None, 3 * N_GROUPS, n_heads, HEAD_DIM), lambda b: (b, 0, 0, 0))] + specs,
        out_specs=pl.BlockSpec((None, n_heads, HEAD_DIM), lambda b: (b, 0, 0)),
        out_shape=jax.ShapeDtypeStruct((B, n_heads, HEAD_DIM), F32),
        compiler_params=_params("parallel"),
        name="sample_attention",
    )(qkv, *views)


def _cache_shift_kernel(cur_ref, nxt_ref, new_ref, out_ref):
    k = pl.program_id(1)
    R = out_ref.shape[0]
    out_ref[0:R - 1] = cur_ref[1:R]
    last = k == pl.num_programs(1) - 1

    @pl.when(last)
    def _():
        out_ref[R - 1] = new_ref[0]

    @pl.when(jnp.logical_not(last))
    def _():
        out_ref[R - 1] = nxt_ref[0]


def cache_shift(cache, new, *, block):
    NL, B, n_buf = cache.shape[:3]
    rows = cache.shape[3] * cache.shape[4]
    c4 = cache.reshape(NL * B, n_buf, rows, HEAD_DIM)
    n4 = new.reshape(NL * B, 1, rows, HEAD_DIM)
    R = min(block, n_buf)
    out = pl.pallas_call(
        _cache_shift_kernel,
        grid=(NL * B, n_buf // R),
        in_specs=[pl.BlockSpec((None, R, rows, HEAD_DIM), lambda lb, k: (lb, k, 0, 0)),
                  pl.BlockSpec((None, 1, rows, HEAD_DIM), lambda lb, k: (lb, jnp.minimum((k + 1) * R, n_buf - 1), 0, 0)),
                  pl.BlockSpec((None, 1, rows, HEAD_DIM), lambda lb, k: (lb, 0, 0, 0))],
        out_specs=pl.BlockSpec((None, R, rows, HEAD_DIM), lambda lb, k: (lb, k, 0, 0)),
        out_shape=jax.ShapeDtypeStruct(c4.shape, c4.dtype),
        compiler_params=_params("parallel", "parallel"),
        name="cache_shift",
    )(c4, c4, n4)
    return out.reshape(cache.shape)


def kernel(x_prompt, x_sample, cache_kv_w128, cache_kv_w512, cache_kv_w2048, state_gla, state_ffn_conv,
           norm_mix, norm_ffn, dsa_w_qkv, dsa_q_gain, dsa_k_gain, dsa_w_o,
           gla_w_in, gla_w_gate2, gla_b_gate, gla_norm_gain, gla_w_o,
           ffn_w_in, ffn_conv_w, ffn_conv_b, ffn_w_out):
    B, S, D = x_prompt.shape
    BS = x_sample.shape[0]
    assert x_sample.shape[1] == 1
    depth = norm_mix.shape[0]
    caches = (cache_kv_w128, cache_kv_w512, cache_kv_w2048)
    n_gla, gla_heads, gla_dk, gla_dv = state_gla.shape[0], state_gla.shape[2], state_gla.shape[3], state_gla.shape[4]
    gla_kd, gla_vd = gla_heads * gla_dk, gla_heads * gla_dv
    dsa_heads = dsa_w_o.shape[1] // HEAD_DIM
    W = dsa_heads * HEAD_DIM
    d_ff = ffn_w_in.shape[2] // 2
    M = B * S
    R = SAMPLE_ROWS

    tm = min(1024, S)
    tn = min(512, W)
    tw = 256
    tm_rms = min(512, M)
    gla_chunk = min(256, S)
    gla_hps = min(4, gla_heads)

    xp = x_prompt.reshape(M, D)
    xs = jnp.pad(x_sample.reshape(BS, D), ((0, R - BS), (0, 0)))

    norm_mix3 = norm_mix.reshape(depth, 1, D)
    norm_ffn3 = norm_ffn.reshape(depth, 1, D)
    conv_b3 = ffn_conv_b.reshape(depth, 1, d_ff)
    n_main = 2 * gla_kd + 2 * gla_vd
    w_gl = jnp.pad(gla_w_in[:, :, n_main:], ((0, 0), (0, 0), (0, LANES - GLA_GATE_RANK)))
    gla_w_in_t = jnp.swapaxes(gla_w_in, 1, 2)
    w_gate2 = jnp.pad(gla_w_gate2, ((0, 0), (0, LANES - GLA_GATE_RANK), (0, 0)))
    b_gate3 = gla_b_gate.reshape(n_gla, 1, gla_kd)
    gla_ng3 = gla_norm_gain.reshape(n_gla, 1, gla_dv)
    ones = jnp.ones_like(dsa_q_gain)
    qk_gains = jnp.stack([dsa_q_gain, dsa_k_gain, ones], axis=2).reshape(-1, 1, HEAD_DIM)
    rope_p = _rope_tables(jnp.arange(S))
    rope_s = _rope_tables(jnp.full((R,), PAST_LEN, jnp.int32))

    qkvh_layers = []
    kv_new = [[] for _ in DSA_GROUPS]
    gla_p, gla_s, conv_p, conv_s = [], [], [], []
    zero_state = jnp.zeros((1, B, gla_heads, gla_dk, gla_dv), F32)

    for i in range(depth):
        j = i // 2
        hp = rms_bf16(xp, norm_mix3, i, tm_rms)
        hs = rms_bf16(xs, norm_mix3, i, R)
        if i % 2 == 0:
            h, h_s = mm_wres(hp, hs, gla_w_in_t, j, tm=tm, tn=tn, ncols=n_main, w_transposed=True, name="gla_in")
            gl, gl_s = mm_wres(hp, hs, w_gl, j, tm=tm, tn=LANES, name="gla_gate_in")
            o, sp = gla_core(h.reshape(B, S, n_main), gl.reshape(B, S, LANES), w_gate2, b_gate3, gla_ng3,
                             zero_state, j, 0, seq_len=S, C=gla_chunk, n_heads=gla_heads, dk=gla_dk, dv=gla_dv,
                             heads_per_step=gla_hps)
            h_s = jnp.pad(h_s[:BS].reshape(BS, 1, n_main), ((0, 0), (0, R - 1), (0, 0)))
            gl_s = jnp.pad(gl_s[:BS].reshape(BS, 1, LANES), ((0, 0), (0, R - 1), (0, 0)))
            o_s, ss = gla_core(h_s, gl_s, w_gate2, b_gate3, gla_ng3, state_gla, j, j,
                               seq_len=1, C=R, n_heads=gla_heads, dk=gla_dk, dv=gla_dv, heads_per_step=gla_hps)
            o_s = jnp.pad(o_s[:, 0], ((0, R - BS), (0, 0)))
            xp, xs = mm_wres(o.reshape(M, gla_vd), o_s, gla_w_o, j, tm=tm, tn=tn, residual=xp, residual_s=xs,
                             name="gla_out")
            gla_p.append(sp)
            gla_s.append(ss)
        else:
            qkvh, qkv_s = dsa_qkv(hp, hs, dsa_w_qkv, qk_gains, rope_p, rope_s, j, tm=tm, tn=tn, section=W, seq=S)
            merged = prompt_attention(qkvh, n_heads=dsa_heads)
            qkvh_layers.append(qkvh)
            qkv_s = qkv_s[:BS].reshape(BS, N_GROUPS, 3, dsa_heads, HEAD_DIM)
            merged_s = sample_attention(qkv_s.reshape(BS, 3 * N_GROUPS, dsa_heads, HEAD_DIM), caches, j,
                                        n_heads=dsa_heads)
            merged_s = jnp.pad(merged_s.reshape(BS, W).astype(BF16), ((0, R - BS), (0, 0)))
            xp, xs = mm_wres(merged.reshape(M, W), merged_s, dsa_w_o, j, tm=tm, tn=tn, residual=xp,
                             residual_s=xs, name="dsa_out")
            for g in range(N_GROUPS):
                kv_new[g].append(qkv_s[:, g, 1:3])
        fp = rms_bf16(xp, norm_ffn3, i, tm_rms)
        fs = rms_bf16(xs, norm_ffn3, i, R)
        st = jnp.pad(state_ffn_conv[i], ((0, R - BS), (0, 0), (0, 0)))
        act, cp, act_s, g_s = ffn_in(fp, fs, ffn_w_in, ffn_conv_w, conv_b3, st[:, 0], st[:, 1], i,
                                     seq=S, tm=tm, tw=tw)
        xp, xs = mm_fullk(act, act_s, ffn_w_out, i, xp, xs, tm=tm, tn=tw, name="ffn_out")
        conv_p.append(cp)
        conv_s.append(jnp.stack([state_ffn_conv[i][:, 1], g_s[:BS]], axis=1))

    kv128_p, kv512_p, kv2048_p = [
        kv_export(qkvh_layers, g, keep=min(window, S), n_heads=dsa_heads, ts=min(256, window, S))
        for g, (window, _) in enumerate(DSA_GROUPS)]
    kv128_s, kv512_s, kv2048_s = [cache_shift(c, jnp.stack(n, axis=0), block=512)
                                  for c, n in zip(caches, kv_new)]
    return (xp.reshape(B, S, D), xs[:BS].reshape(BS, 1, D),
            kv128_p, kv128_s, kv512_p, kv512_s, kv2048_p, kv2048_s,
            jnp.stack(gla_p, axis=0), jnp.stack(gla_s, axis=0),
            jnp.stack(conv_p, axis=0), jnp.stack(conv_s, axis=0))
```

```python
import functools
import math

import numpy as np
import jax
import jax.numpy as jnp
from jax import lax
from jax.experimental import pallas as pl
from jax.experimental.pallas import tpu as pltpu

F32 = jnp.float32
BF16 = jnp.bfloat16

EPS = 1e-6
HEAD_DIM = 128
ROT_DIM = HEAD_DIM // 4
ROPE_THETA = 500000.0
DSA_GROUPS = ((128, 1), (512, 4), (2048, 16))
N_GROUPS = len(DSA_GROUPS)
GLA_GATE_RANK = 16
GLA_GATE_TAU = 16.0
CONV_W = 3
PAST_LEN = 8192
LANES = 128
SUBLANES = 8
SAMPLE_ROWS = 16
VMEM_LIMIT_BYTES = 56 * 1024 * 1024


def _params(*sem):
    return pltpu.CompilerParams(dimension_semantics=sem, vmem_limit_bytes=VMEM_LIMIT_BYTES)


def _rms_kernel(x_ref, g_ref, o_ref):
    x = x_ref[...]
    y = x * lax.rsqrt(jnp.mean(x * x, axis=-1, keepdims=True) + EPS)
    o_ref[...] = (y * g_ref[...]).astype(o_ref.dtype)


def rms_bf16(x, gains, layer, tm):
    M, D = x.shape
    return pl.pallas_call(
        _rms_kernel,
        grid=(M // tm,),
        in_specs=[pl.BlockSpec((tm, D), lambda i: (i, 0)),
                  pl.BlockSpec((None, 1, D), lambda i: (layer, 0, 0))],
        out_specs=pl.BlockSpec((tm, D), lambda i: (i, 0)),
        out_shape=jax.ShapeDtypeStruct((M, D), BF16),
        compiler_params=_params("parallel"),
        name="rms_bf16",
    )(x, gains)


def _emit_normed(x, gain_ref, an_ref, sq_ref):
    an_ref[...] = (x * gain_ref[...]).astype(an_ref.dtype)
    sq_ref[...] = jnp.sum(x * x, axis=-1, keepdims=True)


def _mm_wres_kernel(*refs, n_extra, n_extra_s, has_res, has_norm, epilogue, epilogue_s, w_transposed):
    it = iter(refs)
    a_ref, sc_ref, w_ref = next(it), next(it), next(it)
    extra = [next(it) for _ in range(n_extra)]
    r_ref = next(it) if has_res else None
    gain_ref = next(it) if has_norm else None
    as_ref, scs_ref = next(it), next(it)
    extra_s = [next(it) for _ in range(n_extra_s)]
    rs_ref = next(it) if has_res else None
    o_ref, os_ref = next(it), next(it)
    if has_norm:
        an_ref, ans_ref, sq_ref, sqs_ref = next(it), next(it), next(it), next(it)
    wb_ref = next(it)
    i = pl.program_id(1)

    def product_of(lhs_ref, scale_ref, res_ref):
        def product(rows=slice(None)):
            if w_transposed:
                acc = _nt(lhs_ref[rows, :], wb_ref[...])
            else:
                acc = jnp.dot(lhs_ref[rows, :], wb_ref[...], preferred_element_type=F32)
            acc = acc * scale_ref[rows, :]
            if has_res:
                acc = acc + res_ref[rows, :]
            return acc
        return product

    @pl.when(i == 0)
    def _():
        wb_ref[...] = w_ref[...].astype(BF16)
        product_s = product_of(as_ref, scs_ref, rs_ref)
        if epilogue_s is None:
            x_s = product_s()
            os_ref[...] = x_s.astype(os_ref.dtype)
            if has_norm:
                _emit_normed(x_s, gain_ref, ans_ref, sqs_ref)
        else:
            epilogue_s(product_s, os_ref, *extra_s)

    product = product_of(a_ref, sc_ref, r_ref)
    if epilogue is None:
        x = product()
        o_ref[...] = x.astype(o_ref.dtype)
        if has_norm:
            _emit_normed(x, gain_ref, an_ref, sq_ref)
    else:
        epilogue(product, o_ref, *extra)


def mm_wres(lhs, lhs_s, w, layer, *, tm, tn, col0=0, ncols=None, out_dtype=F32, residual=None, residual_s=None,
            norm=None, epilogue=None, epilogue_s=None, extra=(), extra_specs=(), extra_s=(), extra_specs_s=(),
            out_spec=None, out_shape=None, w_transposed=False, name="mm_wres"):
    (a, scale), (a_s, scale_s) = lhs, lhs_s
    M, K = a.shape
    R = a_s.shape[0]
    ncols = w.shape[1 if w_transposed else 2] - col0 if ncols is None else ncols
    cb0 = col0 // tn
    if w_transposed:
        w_spec = pl.BlockSpec((None, tn, K), lambda j, i: (layer, cb0 + j, 0))
    else:
        w_spec = pl.BlockSpec((None, K, tn), lambda j, i: (layer, 0, cb0 + j))
    if out_spec is None:
        out_spec = pl.BlockSpec((tm, tn), lambda j, i: (i, j))
        out_shape = jax.ShapeDtypeStruct((M, ncols), out_dtype)
    has_res = residual is not None
    has_norm = norm is not None
    ncb = ncols // tn
    in_specs = [pl.BlockSpec((tm, K), lambda j, i: (i, 0)), pl.BlockSpec((tm, 1), lambda j, i: (i, 0)), w_spec]
    in_specs += list(extra_specs)
    args = [a, scale, w, *extra]
    if has_res:
        in_specs.append(pl.BlockSpec((tm, tn), lambda j, i: (i, j)))
        args.append(residual)
    if has_norm:
        gains, norm_layer = norm
        in_specs.append(pl.BlockSpec((None, 1, tn), lambda j, i: (norm_layer, 0, j)))
        args.append(gains)
    in_specs += [pl.BlockSpec((R, K), lambda j, i: (0, 0)), pl.BlockSpec((R, 1), lambda j, i: (0, 0))]
    in_specs += list(extra_specs_s)
    args += [a_s, scale_s, *extra_s]
    if has_res:
        in_specs.append(pl.BlockSpec((R, tn), lambda j, i: (0, j)))
        args.append(residual_s)
    out_specs = [out_spec, pl.BlockSpec((R, tn), lambda j, i: (0, j))]
    out_shapes = [out_shape, jax.ShapeDtypeStruct((R, ncols), out_dtype)]
    if has_norm:
        out_specs += [pl.BlockSpec((tm, tn), lambda j, i: (i, j)), pl.BlockSpec((R, tn), lambda j, i: (0, j)),
                      pl.BlockSpec((None, tm, 1), lambda j, i: (j, i, 0)),
                      pl.BlockSpec((None, R, 1), lambda j, i: (j, 0, 0))]
        out_shapes += [jax.ShapeDtypeStruct((M, ncols), BF16), jax.ShapeDtypeStruct((R, ncols), BF16),
                       jax.ShapeDtypeStruct((ncb, M, 1), F32), jax.ShapeDtypeStruct((ncb, R, 1), F32)]
    kern = functools.partial(_mm_wres_kernel, n_extra=len(extra), n_extra_s=len(extra_s), has_res=has_res,
                             has_norm=has_norm, epilogue=epilogue, epilogue_s=epilogue_s,
                             w_transposed=w_transposed)
    return pl.pallas_call(
        kern,
        grid=(ncb, M // tm),
        in_specs=in_specs,
        out_specs=out_specs,
        out_shape=out_shapes,
        scratch_shapes=[pltpu.VMEM((tn, K) if w_transposed else (K, tn), BF16)],
        compiler_params=_params("parallel", "arbitrary"),
        name=name,
    )(*args)


def _mm_fullk_kernel(*refs, has_norm):
    if has_norm:
        a_ref, w_ref, r_ref, as_ref, rs_ref, gain_ref, o_ref, os_ref, an_ref, ans_ref, sq_ref, sqs_ref = refs
    else:
        a_ref, w_ref, r_ref, as_ref, rs_ref, o_ref, os_ref = refs
    wb = w_ref[...].astype(BF16)
    x = r_ref[...] + jnp.dot(a_ref[...], wb, preferred_element_type=F32)
    o_ref[...] = x
    if has_norm:
        _emit_normed(x, gain_ref, an_ref, sq_ref)

    @pl.when(pl.program_id(0) == 0)
    def _():
        x_s = rs_ref[...] + jnp.dot(as_ref[...], wb, preferred_element_type=F32)
        os_ref[...] = x_s
        if has_norm:
            _emit_normed(x_s, gain_ref, ans_ref, sqs_ref)

    @pl.when(pl.program_id(0) > 0)
    def _():
        os_ref[...] = jnp.zeros_like(os_ref)
        if has_norm:
            ans_ref[...] = jnp.zeros_like(ans_ref)
            sqs_ref[...] = jnp.zeros_like(sqs_ref)


def mm_fullk(a, a_s, w, layer, residual, residual_s, *, tm, tn, norm=None, name="mm_fullk"):
    M, K = a.shape
    R = a_s.shape[0]
    N = w.shape[2]
    nI, ncb = M // tm, N // tn
    has_norm = norm is not None
    in_specs = [pl.BlockSpec((tm, K), lambda i, j: (i, 0), pipeline_mode=pl.Buffered(1)),
                pl.BlockSpec((None, K, tn), lambda i, j: (layer, 0, j)),
                pl.BlockSpec((tm, tn), lambda i, j: (i, j)),
                pl.BlockSpec((R, K), lambda i, j: (0, 0)),
                pl.BlockSpec((R, tn), lambda i, j: (0, j))]
    args = [a, w, residual, a_s, residual_s]
    out_specs = [pl.BlockSpec((tm, tn), lambda i, j: (i, j)),
                 pl.BlockSpec((None, R, tn), lambda i, j: (i, 0, j))]
    out_shapes = [jax.ShapeDtypeStruct((M, N), F32), jax.ShapeDtypeStruct((nI, R, N), F32)]
    if has_norm:
        gains, norm_layer = norm
        in_specs.append(pl.BlockSpec((None, 1, tn), lambda i, j: (norm_layer, 0, j)))
        args.append(gains)
        out_specs += [pl.BlockSpec((tm, tn), lambda i, j: (i, j)),
                      pl.BlockSpec((None, R, tn), lambda i, j: (i, 0, j)),
                      pl.BlockSpec((None, tm, 1), lambda i, j: (j, i, 0)),
                      pl.BlockSpec((None, None, R, 1), lambda i, j: (i, j, 0, 0))]
        out_shapes += [jax.ShapeDtypeStruct((M, N), BF16), jax.ShapeDtypeStruct((nI, R, N), BF16),
                       jax.ShapeDtypeStruct((ncb, M, 1), F32), jax.ShapeDtypeStruct((nI, ncb, R, 1), F32)]
    outs = pl.pallas_call(
        functools.partial(_mm_fullk_kernel, has_norm=has_norm),
        grid=(nI, ncb),
        in_specs=in_specs,
        out_specs=out_specs,
        out_shape=out_shapes,
        compiler_params=_params("arbitrary", "arbitrary"),
        name=name,
    )(*args)
    if has_norm:
        out, out_s, an, an_s, sq, sq_s = outs
        return out, out_s[0], an, an_s[0], sq, sq_s[0]
    return outs[0], outs[1][0]


def _row_scale_kernel(sq_ref, o_ref, *, width):
    o_ref[...] = lax.rsqrt(jnp.sum(sq_ref[...], axis=0) / width + EPS)


def row_scale(sq, *, width, tm):
    ncb, rows, _ = sq.shape
    return pl.pallas_call(
        functools.partial(_row_scale_kernel, width=width),
        grid=(rows // tm,),
        in_specs=[pl.BlockSpec((ncb, tm, 1), lambda i: (0, i, 0))],
        out_specs=pl.BlockSpec((tm, 1), lambda i: (i, 0)),
        out_shape=jax.ShapeDtypeStruct((rows, 1), F32),
        compiler_params=_params("parallel"),
        name="row_scale",
    )(sq)


def _silu(x):
    return x * jax.nn.sigmoid(x)


def _ffn_in_kernel(a_ref, sc_ref, wg_ref, wu_ref, cw_ref, cb_ref, as_ref, scs_ref, s0_ref, s1_ref,
                   act_ref, st_ref, acts_ref, gs_ref, wgb_ref, wub_ref, carry_ref, *, tiles_per_seq):
    i = pl.program_id(1)
    cw = cw_ref[...]
    cb = cb_ref[...]

    @pl.when(i == 0)
    def _():
        wgb_ref[...] = wg_ref[...].astype(BF16)
        wub_ref[...] = wu_ref[...].astype(BF16)
        a_s = as_ref[...]
        g_s = jnp.dot(a_s, wgb_ref[...], preferred_element_type=F32) * scs_ref[...]
        u_s = jnp.dot(a_s, wub_ref[...], preferred_element_type=F32) * scs_ref[...]
        c_s = cb + cw[0:1] * s0_ref[...] + cw[1:2] * s1_ref[...] + cw[2:3] * g_s
        acts_ref[...] = (_silu(c_s) * u_s).astype(acts_ref.dtype)
        gs_ref[...] = g_s

    @pl.when(i % tiles_per_seq == 0)
    def _():
        carry_ref[...] = jnp.zeros_like(carry_ref)

    a = a_ref[...]
    g = jnp.dot(a, wgb_ref[...], preferred_element_type=F32) * sc_ref[...]
    u = jnp.dot(a, wub_ref[...], preferred_element_type=F32) * sc_ref[...]
    tm = g.shape[0]
    prev = carry_ref[...]
    row = lax.broadcasted_iota(jnp.int32, g.shape, 0)
    g1 = jnp.where(row == 0, prev[7:8], pltpu.roll(g, 1, axis=0))
    g2 = jnp.where(row == 0, prev[6:7],
                   jnp.where(row == 1, prev[7:8], pltpu.roll(g, 2, axis=0)))
    c = cb + cw[0:1] * g2 + cw[1:2] * g1 + cw[2:3] * g
    act_ref[...] = (_silu(c) * u).astype(act_ref.dtype)
    carry_ref[...] = g[tm - 8:tm]
    st_ref[...] = g[tm - (CONV_W - 1):tm]


def ffn_in(lhs, lhs_s, w_in, conv_w, conv_b, st0, st1, layer, *, seq, tm, tw):
    (a, scale), (a_s, scale_s) = lhs, lhs_s
    M, K = a.shape
    R = a_s.shape[0]
    d_ff = w_in.shape[2] // 2
    nb = M // seq
    tps = seq // tm
    ub0 = d_ff // tw
    kern = functools.partial(_ffn_in_kernel, tiles_per_seq=tps)
    return pl.pallas_call(
        kern,
        grid=(d_ff // tw, M // tm),
        in_specs=[pl.BlockSpec((tm, K), lambda j, i: (i, 0)),
                  pl.BlockSpec((tm, 1), lambda j, i: (i, 0)),
                  pl.BlockSpec((None, K, tw), lambda j, i: (layer, 0, j)),
                  pl.BlockSpec((None, K, tw), lambda j, i: (layer, 0, ub0 + j)),
                  pl.BlockSpec((None, CONV_W, tw), lambda j, i: (layer, 0, j)),
                  pl.BlockSpec((None, 1, tw), lambda j, i: (layer, 0, j)),
                  pl.BlockSpec((R, K), lambda j, i: (0, 0)),
                  pl.BlockSpec((R, 1), lambda j, i: (0, 0)),
                  pl.BlockSpec((R, tw), lambda j, i: (0, j)),
                  pl.BlockSpec((R, tw), lambda j, i: (0, j))],
        out_specs=[pl.BlockSpec((tm, tw), lambda j, i: (i, j)),
                   pl.BlockSpec((None, CONV_W - 1, tw), lambda j, i: (i // tps, 0, j)),
                   pl.BlockSpec((R, tw), lambda j, i: (0, j)),
                   pl.BlockSpec((R, tw), lambda j, i: (0, j))],
        out_shape=[jax.ShapeDtypeStruct((M, d_ff), BF16),
                   jax.ShapeDtypeStruct((nb, CONV_W - 1, d_ff), F32),
                   jax.ShapeDtypeStruct((R, d_ff), BF16),
                   jax.ShapeDtypeStruct((R, d_ff), F32)],
        scratch_shapes=[pltpu.VMEM((K, tw), BF16), pltpu.VMEM((K, tw), BF16),
                        pltpu.VMEM((8, tw), F32)],
        compiler_params=_params("parallel", "arbitrary"),
        name="ffn_in",
    )(a, scale, w_in, w_in, conv_w, conv_b, a_s, scale_s, st0, st1)


def _gla_tables(C):
    n_levels = int(math.log2(C))
    assert 1 << n_levels == C
    i = np.arange(C)[:, None]
    t = np.arange(C)[None, :]
    mats = [t <= i]
    level = np.full((C, C), -1, np.int32)
    level[np.arange(C), np.arange(C)] = n_levels
    for lv in range(n_levels):
        s = 1 << lv
        mid = (i // (2 * s)) * (2 * s) + s
        upper = i >= mid
        if 2 * s <= SUBLANES:
            mats.append((upper & (t >= mid) & (t <= i)) | (~upper & (t > i) & (t < mid)))
        same = (i // (2 * s)) == (t // (2 * s))
        own = same & upper & ~((t % (2 * s)) >= s)
        level[own] = lv
    return np.concatenate(mats, axis=0).astype(np.float32), level, n_levels


def _nt(a, b):
    return lax.dot_general(a, b, (((1,), (1,)), ((), ())), preferred_element_type=F32)


def _tn(a, b):
    return lax.dot_general(a, b, (((0,), (0,)), ((), ())), preferred_element_type=F32)


def _gla_kernel(q_ref, k_ref, v_ref, r_ref, gl_ref, w2_ref, bg_ref, ng_ref, s0_ref, ms_ref, lvl_ref,
                o_ref, sout_ref, st_ref, *, C, n_levels, seq_len, dk, dv, heads):
    c = pl.program_id(2)
    nc = pl.num_programs(2)

    @pl.when(c == 0)
    def _():
        for hh in range(heads):
            st_ref[hh] = s0_ref[hh].T

    glb = gl_ref[...].astype(BF16)
    ms = ms_ref[...]
    lvl = lvl_ref[...]
    for hh in range(heads):
        ks = slice(hh * dk, (hh + 1) * dk)
        vs = slice(hh * dv, (hh + 1) * dv)
        z = jnp.dot(glb, w2_ref[:, ks].astype(BF16), preferred_element_type=F32) + bg_ref[:, ks]
        g = -(jnp.maximum(-z, 0.0) + jnp.log1p(jnp.exp(-jnp.abs(z)))) / GLA_GATE_TAU
        if seq_len % C:
            row = c * C + lax.broadcasted_iota(jnp.int32, g.shape, 0)
            g = jnp.where(row < seq_len, g, 0.0)
        g_hi = g.astype(BF16)
        g_lo = (g - g_hi.astype(F32)).astype(BF16)
        ex = jnp.dot(ms, g_hi, preferred_element_type=F32) + jnp.dot(ms, g_lo, preferred_element_type=F32)
        b = ex[0:C]
        e_b = jnp.exp(b)
        e_end = jnp.exp(b[C - 1:C] - b)

        q = q_ref[:, ks] * (dk ** -0.5)
        k = k_ref[:, ks]
        vb = v_ref[:, vs].astype(BF16)
        st = st_ref[hh]
        o = _nt((q * e_b).astype(BF16), st.astype(BF16))
        a = jnp.where(lvl == n_levels, _nt(q.astype(BF16), k.astype(BF16)), 0.0)
        for lv in range(n_levels):
            s = 1 << lv
            if 2 * s <= SUBLANES:
                e_l = jnp.exp(ex[(1 + lv) * C:(2 + lv) * C])
            else:
                m = jnp.broadcast_to(b.reshape(C // (2 * s), 2 * s, dk)[:, s - 1:s, :], (C // (2 * s), 2 * s, dk))
                e_l = jnp.exp(-jnp.abs(b - m.reshape(C, dk)))
            a = jnp.where(lvl == lv, _nt((q * e_l).astype(BF16), (k * e_l).astype(BF16)), a)
        o = o + jnp.dot(a.astype(BF16), vb, preferred_element_type=F32)
        decay_end = e_b[C - 1:C]
        st_ref[hh] = st * decay_end + _tn(vb, (k * e_end).astype(BF16))

        y = o * lax.rsqrt(jnp.mean(o * o, axis=-1, keepdims=True) + EPS) * ng_ref[...]
        o_ref[:, vs] = (y * _silu(r_ref[:, vs])).astype(o_ref.dtype)

    @pl.when(c == nc - 1)
    def _():
        for hh in range(heads):
            sout_ref[hh] = st_ref[hh].T


def gla_core(h, gl, w_gate2, b_gate, norm_gain, s0, layer, s0_layer, *, seq_len, C, n_heads, dk, dv,
             heads_per_step):
    B, Lp, _ = h.shape
    ms_np, lvl_np, n_levels = _gla_tables(C)
    ms = jnp.asarray(ms_np, BF16)
    lvl = jnp.asarray(lvl_np)
    hps = heads_per_step
    assert n_heads % hps == 0
    ng = n_heads // hps
    v0 = 2 * n_heads * dk // (hps * dv)
    r0 = v0 + ng
    kern = functools.partial(_gla_kernel, C=C, n_levels=n_levels, seq_len=seq_len, dk=dk, dv=dv, heads=hps)
    R = ms_np.shape[0]
    return pl.pallas_call(
        kern,
        grid=(B, ng, Lp // C),
        in_specs=[pl.BlockSpec((None, C, hps * dk), lambda b, hd, c: (b, c, hd)),
                  pl.BlockSpec((None, C, hps * dk), lambda b, hd, c: (b, c, ng + hd)),
                  pl.BlockSpec((None, C, hps * dv), lambda b, hd, c: (b, c, v0 + hd)),
                  pl.BlockSpec((None, C, hps * dv), lambda b, hd, c: (b, c, r0 + hd)),
                  pl.BlockSpec((None, C, LANES), lambda b, hd, c: (b, c, 0)),
                  pl.BlockSpec((None, LANES, hps * dk), lambda b, hd, c: (layer, 0, hd)),
                  pl.BlockSpec((None, 1, hps * dk), lambda b, hd, c: (layer, 0, hd)),
                  pl.BlockSpec((None, 1, dv), lambda b, hd, c: (layer, 0, 0)),
                  pl.BlockSpec((None, None, hps, dk, dv), lambda b, hd, c: (s0_layer, b, hd, 0, 0)),
                  pl.BlockSpec((R, C), lambda b, hd, c: (0, 0)),
                  pl.BlockSpec((C, C), lambda b, hd, c: (0, 0))],
        out_specs=[pl.BlockSpec((None, C, hps * dv), lambda b, hd, c: (b, c, hd)),
                   pl.BlockSpec((None, hps, dk, dv), lambda b, hd, c: (b, hd, 0, 0))],
        out_shape=[jax.ShapeDtypeStruct((B, Lp, n_heads * dv), BF16),
                   jax.ShapeDtypeStruct((B, n_heads, dk, dv), F32)],
        scratch_shapes=[pltpu.VMEM((hps, dv, dk), F32)],
        compiler_params=_params("parallel", "parallel", "arbitrary"),
        name="gla_core",
    )(h, h, h, h, gl, w_gate2, b_gate, norm_gain, s0, ms, lvl)


def _rope_tables(pos):
    half = ROT_DIM // 2
    inv = jnp.power(ROPE_THETA, -jnp.arange(half, dtype=F32) * (2.0 / ROT_DIM))
    ang = pos.astype(F32)[:, None] * inv[None, :]
    cos, sin = jnp.cos(ang), jnp.sin(ang)
    n = pos.shape[0]
    cos_t = jnp.concatenate([cos, cos, jnp.ones((n, HEAD_DIM - ROT_DIM), F32)], axis=1)
    sin_t = jnp.concatenate([-sin, sin, jnp.zeros((n, HEAD_DIM - ROT_DIM), F32)], axis=1)
    return cos_t, sin_t


def _qkv_epilogue(product, o_ref, gain_ref, cos_ref, sin_ref, *, tn, section, head_major, row_chunk):
    j = pl.program_id(0)
    kind = (j * tn // section) % 3
    tm = cos_ref.shape[0]
    chunks = [slice(r, r + row_chunk) for r in range(0, tm, row_chunk)]

    def put(rows, hh, val):
        if head_major:
            o_ref[hh, rows, :] = val
        else:
            o_ref[rows, hh * HEAD_DIM:(hh + 1) * HEAD_DIM] = val

    @pl.when(kind == 2)
    def _():
        for rows in chunks:
            acc = product(rows)
            for hh in range(tn // HEAD_DIM):
                put(rows, hh, acc[:, hh * HEAD_DIM:(hh + 1) * HEAD_DIM])

    @pl.when(kind != 2)
    def _():
        half = ROT_DIM // 2
        gain = gain_ref[...]
        lane = lax.broadcasted_iota(jnp.int32, (row_chunk, HEAD_DIM), 1)
        for rows in chunks:
            acc = product(rows)
            cos_t = cos_ref[rows, :]
            sin_t = sin_ref[rows, :]
            for hh in range(tn // HEAD_DIM):
                x = acc[:, hh * HEAD_DIM:(hh + 1) * HEAD_DIM]
                y = x * lax.rsqrt(jnp.mean(x * x, axis=-1, keepdims=True) + EPS) * gain
                swapped = jnp.where(lane < half, pltpu.roll(y, HEAD_DIM - half, axis=1),
                                    pltpu.roll(y, half, axis=1))
                put(rows, hh, y * cos_t + swapped * sin_t)


def dsa_qkv(lhs, lhs_s, w_qkv, gains, rope, rope_s, layer, *, tm, tn, section, seq):
    M = lhs[0].shape[0]
    R = lhs_s[0].shape[0]
    tps = seq // tm
    n_sec = w_qkv.shape[2] // section
    gain_spec = pl.BlockSpec((None, 1, HEAD_DIM), lambda j, i: (layer * n_sec + j * tn // section, 0, 0))
    extra_specs = [gain_spec,
                   pl.BlockSpec((tm, HEAD_DIM), lambda j, i: (i % tps, 0)),
                   pl.BlockSpec((tm, HEAD_DIM), lambda j, i: (i % tps, 0))]
    extra_specs_s = [gain_spec,
                     pl.BlockSpec((R, HEAD_DIM), lambda j, i: (0, 0)),
                     pl.BlockSpec((R, HEAD_DIM), lambda j, i: (0, 0))]
    epi = functools.partial(_qkv_epilogue, tn=tn, section=section, head_major=True, row_chunk=min(256, tm))
    epi_s = functools.partial(_qkv_epilogue, tn=tn, section=section, head_major=False, row_chunk=R)
    hpb = tn // HEAD_DIM
    out_spec = pl.BlockSpec((None, hpb, tm, HEAD_DIM), lambda j, i: (i // tps, j, i % tps, 0))
    out_shape = jax.ShapeDtypeStruct((M // seq, w_qkv.shape[2] // HEAD_DIM, seq, HEAD_DIM), F32)
    return mm_wres(lhs, lhs_s, w_qkv, layer, tm=tm, tn=tn, epilogue=epi, epilogue_s=epi_s,
                   extra=(gains, *rope), extra_specs=extra_specs,
                   extra_s=(gains, *rope_s), extra_specs_s=extra_specs_s,
                   out_spec=out_spec, out_shape=out_shape, name="dsa_qkv")


def _rows(start, size, stride):
    if stride == 1:
        return (pl.ds(start, size), slice(None))
    return (pl.ds(start, size, stride=stride), slice(None))


def _prompt_attn_kernel(*refs, seq, merge_rows):
    qkv_refs = refs[:3 * N_GROUPS]
    out_ref = refs[3 * N_GROUPS]
    o_scr = refs[3 * N_GROUPS + 1:3 * N_GROUPS + 1 + N_GROUPS]
    l_scr = refs[3 * N_GROUPS + 1 + N_GROUPS:]
    scale = HEAD_DIM ** -0.5
    for g, (window, dil) in enumerate(DSA_GROUPS):
        q_ref, k_ref, v_ref = qkv_refs[3 * g:3 * g + 3]
        blk = window // dil
        span = blk * dil
        nb = seq // span
        qi = lax.broadcasted_iota(jnp.int32, (blk, 2 * blk), 0)
        ki = lax.broadcasted_iota(jnp.int32, (blk, 2 * blk), 1)
        both_ok = (ki >= qi) & (ki <= qi + blk)
        first_ok = (lax.broadcasted_iota(jnp.int32, (blk, blk), 1)
                    <= lax.broadcasted_iota(jnp.int32, (blk, blk), 0))
        for r in range(dil):
            for n in range(nb):
                start = r + n * span
                q = q_ref[_rows(start, blk, dil)].astype(BF16)
                if n == 0:
                    kk = k_ref[_rows(start, blk, dil)].astype(BF16)
                    vv = v_ref[_rows(start, blk, dil)].astype(BF16)
                    ok = first_ok
                else:
                    kk = k_ref[_rows(start - span, 2 * blk, dil)].astype(BF16)
                    vv = v_ref[_rows(start - span, 2 * blk, dil)].astype(BF16)
                    ok = both_ok
                s = jnp.where(ok, _nt(q, kk) * scale, -jnp.inf)
                mx = jnp.max(s, axis=-1, keepdims=True)
                p = jnp.exp(s - mx)
                den = jnp.sum(p, axis=-1, keepdims=True)
                o = jnp.dot(p.astype(BF16), vv, preferred_element_type=F32) / den
                o_scr[g][_rows(start, blk, dil)] = o
                l_scr[g][_rows(start, blk, dil)] = jnp.broadcast_to(mx + jnp.log(den), (blk, HEAD_DIM))
    for c in range(seq // merge_rows):
        sl = slice(c * merge_rows, (c + 1) * merge_rows)
        ls = [l_scr[g][sl, :] for g in range(N_GROUPS)]
        mx = functools.reduce(jnp.maximum, ls)
        es = [jnp.exp(l - mx) for l in ls]
        den = functools.reduce(lambda a, b: a + b, es)
        acc = es[0] * o_scr[0][sl, :]
        for g in range(1, N_GROUPS):
            acc = acc + es[g] * o_scr[g][sl, :]
        out_ref[sl, :] = (acc / den).astype(out_ref.dtype)


def prompt_attention(qkvh, *, n_heads):
    B, _, S, _ = qkvh.shape
    for window, dil in DSA_GROUPS:
        assert S % window == 0
    in_specs = []
    for g in range(N_GROUPS):
        for t in range(3):
            in_specs.append(pl.BlockSpec((None, None, S, HEAD_DIM),
                                         lambda b, h, g=g, t=t: (b, (g * 3 + t) * n_heads + h, 0, 0)))
    kern = functools.partial(_prompt_attn_kernel, seq=S, merge_rows=min(256, S))
    return pl.pallas_call(
        kern,
        grid=(B, n_heads),
        in_specs=in_specs,
        out_specs=pl.BlockSpec((None, S, HEAD_DIM), lambda b, h: (b, 0, h)),
        out_shape=jax.ShapeDtypeStruct((B, S, n_heads * HEAD_DIM), BF16),
        scratch_shapes=[pltpu.VMEM((S, HEAD_DIM), F32)] * (2 * N_GROUPS),
        compiler_params=_params("parallel", "parallel"),
        name="prompt_attention",
    )(*([qkvh] * (3 * N_GROUPS)))


def _kv_export_kernel(*refs, n_layers, n_heads, ts):
    out_ref = refs[2 * n_layers]
    layer = pl.program_id(0)
    for l in range(n_layers):
        k_ref, v_ref = refs[2 * l], refs[2 * l + 1]

        @pl.when(layer == l)
        def _():
            kt = jnp.swapaxes(k_ref[...], 0, 1)
            vt = jnp.swapaxes(v_ref[...], 0, 1)
            out_ref[...] = jnp.concatenate([kt, vt], axis=1).reshape(ts * 2 * n_heads, HEAD_DIM)


def kv_export(qkvh_layers, group, *, keep, n_heads, ts):
    n_layers = len(qkvh_layers)
    B, _, S, _ = qkvh_layers[0].shape
    nt = keep // ts
    t0 = (S - keep) // ts
    in_specs, args = [], []
    for l, arr in enumerate(qkvh_layers):
        for t in (1, 2):
            def idx(ll, b, i, l=l, t=t):
                bb = jnp.where(ll == l, b, jnp.where(ll < l, 0, B - 1))
                ii = jnp.where(ll == l, i, jnp.where(ll < l, 0, nt - 1))
                return (bb, group * 3 + t, t0 + ii, 0)
            in_specs.append(pl.BlockSpec((None, n_heads, ts, HEAD_DIM), idx))
            args.append(arr)
    rows = 2 * n_heads
    out = pl.pallas_call(
        functools.partial(_kv_export_kernel, n_layers=n_layers, n_heads=n_heads, ts=ts),
        grid=(n_layers, B, nt),
        in_specs=in_specs,
        out_specs=pl.BlockSpec((None, None, ts * rows, HEAD_DIM), lambda ll, b, i: (ll, b, i, 0)),
        out_shape=jax.ShapeDtypeStruct((n_layers, B, keep * rows, HEAD_DIM), F32),
        compiler_params=_params("arbitrary", "arbitrary", "arbitrary"),
        name=f"kv_export_g{group}",
    )(*args)
    return out.reshape(n_layers, B, keep, 2, n_heads, HEAD_DIM)


def _sample_attn_kernel(qkv_ref, c0_ref, c1_ref, c2_ref, out_ref, *, n_heads):
    scale = HEAD_DIM ** -0.5
    os, ls = [], []
    for g, c_ref in enumerate((c0_ref, c1_ref, c2_ref)):
        q, k_new, v_new = qkv_ref[3 * g], qkv_ref[3 * g + 1], qkv_ref[3 * g + 2]
        kc = c_ref[:, 0:n_heads, :]
        vc = c_ref[:, n_heads:2 * n_heads, :]
        s_c = jnp.sum(kc * q[None], axis=-1, keepdims=True) * scale
        s_n = jnp.sum(k_new * q, axis=-1, keepdims=True) * scale
        mx = jnp.maximum(jnp.max(s_c, axis=0), s_n)
        p_c = jnp.exp(s_c - mx[None])
        p_n = jnp.exp(s_n - mx)
        den = jnp.sum(p_c, axis=0) + p_n
        os.append((jnp.sum(p_c * vc, axis=0) + p_n * v_new) / den)
        ls.append(mx + jnp.log(den))
    mx = functools.reduce(jnp.maximum, ls)
    es = [jnp.exp(l - mx) for l in ls]
    den = functools.reduce(lambda a, b: a + b, es)
    acc = es[0] * os[0]
    for g in range(1, N_GROUPS):
        acc = acc + es[g] * os[g]
    out_ref[...] = (acc / den).astype(out_ref.dtype)


def sample_attention(qkv, caches, layer, *, n_heads):
    B = qkv.shape[0]
    rows = 2 * n_heads
    views, specs = [], []
    for (window, dil), c in zip(DSA_GROUPS, caches):
        n_buf = c.shape[2]
        assert n_buf == window and n_buf % dil == 0
        views.append(c.reshape(c.shape[0], B, n_buf // dil, dil * rows, HEAD_DIM))
        specs.append(pl.BlockSpec((None, None, n_buf // dil, rows, HEAD_DIM), lambda b: (layer, b, 0, 0, 0)))
    return pl.pallas_call(
        functools.partial(_sample_attn_kernel, n_heads=n_heads),
        grid=(B,),
        in_specs=[pl.BlockSpec((None, 3 * N_GROUPS, n_heads, HEAD_DIM), lambda b: (b, 0, 0, 0))] + specs,
        out_specs=pl.BlockSpec((None, n_heads, HEAD_DIM), lambda b: (b, 0, 0)),
        out_shape=jax.ShapeDtypeStruct((B, n_heads, HEAD_DIM), F32),
        compiler_params=_params("parallel"),
        name="sample_attention",
    )(qkv, *views)


def _cache_shift_kernel(cur_ref, nxt_ref, new_ref, out_ref):
    k = pl.program_id(1)
    R = out_ref.shape[0]
    out_ref[0:R - 1] = cur_ref[1:R]
    last = k == pl.num_programs(1) - 1

    @pl.when(last)
    def _():
        out_ref[R - 1] = new_ref[0]

    @pl.when(jnp.logical_not(last))
    def _():
        out_ref[R - 1] = nxt_ref[0]


def cache_shift(cache, new, *, block):
    NL, B, n_buf = cache.shape[:3]
    rows = cache.shape[3] * cache.shape[4]
    c4 = cache.reshape(NL * B, n_buf, rows, HEAD_DIM)
    n4 = new.reshape(NL * B, 1, rows, HEAD_DIM)
    R = min(block, n_buf)
    out = pl.pallas_call(
        _cache_shift_kernel,
        grid=(NL * B, n_buf // R),
        in_specs=[pl.BlockSpec((None, R, rows, HEAD_DIM), lambda lb, k: (lb, k, 0, 0)),
                  pl.BlockSpec((None, 1, rows, HEAD_DIM), lambda lb, k: (lb, jnp.minimum((k + 1) * R, n_buf - 1), 0, 0)),
                  pl.BlockSpec((None, 1, rows, HEAD_DIM), lambda lb, k: (lb, 0, 0, 0))],
        out_specs=pl.BlockSpec((None, R, rows, HEAD_DIM), lambda lb, k: (lb, k, 0, 0)),
        out_shape=jax.ShapeDtypeStruct(c4.shape, c4.dtype),
        compiler_params=_params("parallel", "parallel"),
        name="cache_shift",
    )(c4, c4, n4)
    return out.reshape(cache.shape)


def kernel(x_prompt, x_sample, cache_kv_w128, cache_kv_w512, cache_kv_w2048, state_gla, state_ffn_conv,
           norm_mix, norm_ffn, dsa_w_qkv, dsa_q_gain, dsa_k_gain, dsa_w_o,
           gla_w_in, gla_w_gate2, gla_b_gate, gla_norm_gain, gla_w_o,
           ffn_w_in, ffn_conv_w, ffn_conv_b, ffn_w_out):
    B, S, D = x_prompt.shape
    BS = x_sample.shape[0]
    assert x_sample.shape[1] == 1
    depth = norm_mix.shape[0]
    caches = (cache_kv_w128, cache_kv_w512, cache_kv_w2048)
    n_gla, gla_heads, gla_dk, gla_dv = state_gla.shape[0], state_gla.shape[2], state_gla.shape[3], state_gla.shape[4]
    gla_kd, gla_vd = gla_heads * gla_dk, gla_heads * gla_dv
    dsa_heads = dsa_w_o.shape[1] // HEAD_DIM
    W = dsa_heads * HEAD_DIM
    d_ff = ffn_w_in.shape[2] // 2
    M = B * S
    R = SAMPLE_ROWS

    tm = min(1024, S)
    tn = min(512, W)
    tw = 256
    tm_rms = min(512, M)
    gla_chunk = min(256, S)
    gla_hps = min(4, gla_heads)

    xp = x_prompt.reshape(M, D)
    xs = jnp.pad(x_sample.reshape(BS, D), ((0, R - BS), (0, 0)))

    norm_mix3 = norm_mix.reshape(depth, 1, D)
    norm_ffn3 = norm_ffn.reshape(depth, 1, D)
    conv_b3 = ffn_conv_b.reshape(depth, 1, d_ff)
    n_main = 2 * gla_kd + 2 * gla_vd
    w_gl = jnp.pad(gla_w_in[:, :, n_main:], ((0, 0), (0, 0), (0, LANES - GLA_GATE_RANK)))
    gla_w_in_t = jnp.swapaxes(gla_w_in, 1, 2)
    w_gate2 = jnp.pad(gla_w_gate2, ((0, 0), (0, LANES - GLA_GATE_RANK), (0, 0)))
    b_gate3 = gla_b_gate.reshape(n_gla, 1, gla_kd)
    gla_ng3 = gla_norm_gain.reshape(n_gla, 1, gla_dv)
    ones = jnp.ones_like(dsa_q_gain)
    qk_gains = jnp.stack([dsa_q_gain, dsa_k_gain, ones], axis=2).reshape(-1, 1, HEAD_DIM)
    rope_p = _rope_tables(jnp.arange(S))
    rope_s = _rope_tables(jnp.full((R,), PAST_LEN, jnp.int32))

    qkvh_layers = []
    kv_new = [[] for _ in DSA_GROUPS]
    gla_p, gla_s, conv_p, conv_s = [], [], [], []
    zero_state = jnp.zeros((1, B, gla_heads, gla_dk, gla_dv), F32)

    hp = (rms_bf16(xp, norm_mix3, 0, tm_rms), jnp.ones((M, 1), F32))
    hs = (rms_bf16(xs, norm_mix3, 0, R), jnp.ones((R, 1), F32))
    unit, unit_s = jnp.ones((M, 1), F32), jnp.ones((R, 1), F32)

    def normed(an, an_s, sq, sq_s):
        return (an, row_scale(sq, width=D, tm=tm)), (an_s, row_scale(sq_s, width=D, tm=R))

    for i in range(depth):
        j = i // 2
        ffn_norm = (norm_ffn3, i)
        if i % 2 == 0:
            h, h_s = mm_wres(hp, hs, gla_w_in_t, j, tm=tm, tn=tn, ncols=n_main, w_transposed=True, name="gla_in")
            gl, gl_s = mm_wres(hp, hs, w_gl, j, tm=tm, tn=LANES, name="gla_gate_in")
            o, sp = gla_core(h.reshape(B, S, n_main), gl.reshape(B, S, LANES), w_gate2, b_gate3, gla_ng3,
                             zero_state, j, 0, seq_len=S, C=gla_chunk, n_heads=gla_heads, dk=gla_dk, dv=gla_dv,
                             heads_per_step=gla_hps)
            h_s = jnp.pad(h_s[:BS].reshape(BS, 1, n_main), ((0, 0), (0, R - 1), (0, 0)))
            gl_s = jnp.pad(gl_s[:BS].reshape(BS, 1, LANES), ((0, 0), (0, R - 1), (0, 0)))
            o_s, ss = gla_core(h_s, gl_s, w_gate2, b_gate3, gla_ng3, state_gla, j, j,
                               seq_len=1, C=R, n_heads=gla_heads, dk=gla_dk, dv=gla_dv, heads_per_step=gla_hps)
            o_s = jnp.pad(o_s[:, 0], ((0, R - BS), (0, 0)))
            xp, xs, *nrm = mm_wres((o.reshape(M, gla_vd), unit), (o_s, unit_s), gla_w_o, j, tm=tm, tn=tn,
                                   residual=xp, residual_s=xs, norm=ffn_norm, name="gla_out")
            gla_p.append(sp)
            gla_s.append(ss)
        else:
            qkvh, qkv_s = dsa_qkv(hp, hs, dsa_w_qkv, qk_gains, rope_p, rope_s, j, tm=tm, tn=tn, section=W, seq=S)
            merged = prompt_attention(qkvh, n_heads=dsa_heads)
            qkvh_layers.append(qkvh)
            qkv_s = qkv_s[:BS].reshape(BS, N_GROUPS, 3, dsa_heads, HEAD_DIM)
            merged_s = sample_attention(qkv_s.reshape(BS, 3 * N_GROUPS, dsa_heads, HEAD_DIM), caches, j,
                                        n_heads=dsa_heads)
            merged_s = jnp.pad(merged_s.reshape(BS, W).astype(BF16), ((0, R - BS), (0, 0)))
            xp, xs, *nrm = mm_wres((merged.reshape(M, W), unit), (merged_s, unit_s), dsa_w_o, j, tm=tm, tn=tn,
                                   residual=xp, residual_s=xs, norm=ffn_norm, name="dsa_out")
            for g in range(N_GROUPS):
                kv_new[g].append(qkv_s[:, g, 1:3])
        fp, fs = normed(*nrm)
        st = jnp.pad(state_ffn_conv[i], ((0, R - BS), (0, 0), (0, 0)))
        act, cp, act_s, g_s = ffn_in(fp, fs, ffn_w_in, ffn_conv_w, conv_b3, st[:, 0], st[:, 1], i,
                                     seq=S, tm=tm, tw=tw)
        if i + 1 < depth:
            xp, xs, *nrm = mm_fullk(act, act_s, ffn_w_out, i, xp, xs, tm=tm, tn=tw, norm=(norm_mix3, i + 1),
                                    name="ffn_out")
            hp, hs = normed(*nrm)
        else:
            xp, xs = mm_fullk(act, act_s, ffn_w_out, i, xp, xs, tm=tm, tn=tw, name="ffn_out")
        conv_p.append(cp)
        conv_s.append(jnp.stack([state_ffn_conv[i][:, 1], g_s[:BS]], axis=1))

    kv128_p, kv512_p, kv2048_p = [
        kv_export(qkvh_layers, g, keep=min(window, S), n_heads=dsa_heads, ts=min(256, window, S))
        for g, (window, _) in enumerate(DSA_GROUPS)]
    kv128_s, kv512_s, kv2048_s = [cache_shift(c, jnp.stack(n, axis=0), block=512)
                                  for c, n in zip(caches, kv_new)]
    return (xp.reshape(B, S, D), xs[:BS].reshape(BS, 1, D),
            kv128_p, kv128_s, kv512_p, kv512_s, kv2048_p, kv2048_s,
            jnp.stack(gla_p, axis=0), jnp.stack(gla_s, axis=0),
            jnp.stack(conv_p, axis=0), jnp.stack(conv_s, axis=0))
```

```python
import functools
import math

import numpy as np
import jax
import jax.numpy as jnp
from jax import lax
from jax.experimental import pallas as pl
from jax.experimental.pallas import tpu as pltpu
from jax.experimental.pallas import tpu_sc as plsc

F32 = jnp.float32
BF16 = jnp.bfloat16

EPS = 1e-6
HEAD_DIM = 128
ROT_DIM = HEAD_DIM // 4
ROPE_THETA = 500000.0
DSA_GROUPS = ((128, 1), (512, 4), (2048, 16))
N_GROUPS = len(DSA_GROUPS)
GLA_GATE_RANK = 16
GLA_GATE_TAU = 16.0
CONV_W = 3
PAST_LEN = 8192
LANES = 128
SUBLANES = 8
SAMPLE_ROWS = 16
VMEM_LIMIT_BYTES = 56 * 1024 * 1024


def _params(*sem):
    return pltpu.CompilerParams(dimension_semantics=sem, vmem_limit_bytes=VMEM_LIMIT_BYTES)


def _rms_kernel(x_ref, g_ref, o_ref):
    x = x_ref[...]
    y = x * lax.rsqrt(jnp.mean(x * x, axis=-1, keepdims=True) + EPS)
    o_ref[...] = (y * g_ref[...]).astype(o_ref.dtype)


def rms_bf16(x, gains, layer, tm):
    M, D = x.shape
    return pl.pallas_call(
        _rms_kernel,
        grid=(M // tm,),
        in_specs=[pl.BlockSpec((tm, D), lambda i: (i, 0)),
                  pl.BlockSpec((None, 1, D), lambda i: (layer, 0, 0))],
        out_specs=pl.BlockSpec((tm, D), lambda i: (i, 0)),
        out_shape=jax.ShapeDtypeStruct((M, D), BF16),
        compiler_params=_params("parallel"),
        name="rms_bf16",
    )(x, gains)


def _mm_wres_kernel(*refs, n_extra, n_extra_s, has_res, epilogue, epilogue_s, w_transposed):
    it = iter(refs)
    a_ref, w_ref = next(it), next(it)
    extra = [next(it) for _ in range(n_extra)]
    r_ref = next(it) if has_res else None
    as_ref = next(it)
    extra_s = [next(it) for _ in range(n_extra_s)]
    rs_ref = next(it) if has_res else None
    o_ref, os_ref, wb_ref = next(it), next(it), next(it)
    i = pl.program_id(1)

    def product_of(lhs_ref, res_ref):
        def product(rows=slice(None)):
            if w_transposed:
                acc = _nt(lhs_ref[rows, :], wb_ref[...])
            else:
                acc = jnp.dot(lhs_ref[rows, :], wb_ref[...], preferred_element_type=F32)
            if has_res:
                acc = acc + res_ref[rows, :]
            return acc
        return product

    @pl.when(i == 0)
    def _():
        wb_ref[...] = w_ref[...].astype(BF16)
        if epilogue_s is None:
            os_ref[...] = product_of(as_ref, rs_ref)().astype(os_ref.dtype)
        else:
            epilogue_s(product_of(as_ref, rs_ref), os_ref, *extra_s)

    if epilogue is None:
        o_ref[...] = product_of(a_ref, r_ref)().astype(o_ref.dtype)
    else:
        epilogue(product_of(a_ref, r_ref), o_ref, *extra)


def mm_wres(a, a_s, w, layer, *, tm, tn, col0=0, ncols=None, out_dtype=F32, residual=None, residual_s=None,
            epilogue=None, epilogue_s=None, extra=(), extra_specs=(), extra_s=(), extra_specs_s=(),
            out_spec=None, out_shape=None, w_transposed=False, name="mm_wres"):
    M, K = a.shape
    R = a_s.shape[0]
    ncols = w.shape[1 if w_transposed else 2] - col0 if ncols is None else ncols
    cb0 = col0 // tn
    if w_transposed:
        w_spec = pl.BlockSpec((None, tn, K), lambda j, i: (layer, cb0 + j, 0))
    else:
        w_spec = pl.BlockSpec((None, K, tn), lambda j, i: (layer, 0, cb0 + j))
    if out_spec is None:
        out_spec = pl.BlockSpec((tm, tn), lambda j, i: (i, j))
        out_shape = jax.ShapeDtypeStruct((M, ncols), out_dtype)
    has_res = residual is not None
    in_specs = [pl.BlockSpec((tm, K), lambda j, i: (i, 0)), w_spec]
    in_specs += list(extra_specs)
    args = [a, w, *extra]
    if has_res:
        in_specs.append(pl.BlockSpec((tm, tn), lambda j, i: (i, j)))
        args.append(residual)
    in_specs.append(pl.BlockSpec((R, K), lambda j, i: (0, 0)))
    in_specs += list(extra_specs_s)
    args += [a_s, *extra_s]
    if has_res:
        in_specs.append(pl.BlockSpec((R, tn), lambda j, i: (0, j)))
        args.append(residual_s)
    kern = functools.partial(_mm_wres_kernel, n_extra=len(extra), n_extra_s=len(extra_s), has_res=has_res,
                             epilogue=epilogue, epilogue_s=epilogue_s, w_transposed=w_transposed)
    return pl.pallas_call(
        kern,
        grid=(ncols // tn, M // tm),
        in_specs=in_specs,
        out_specs=[out_spec, pl.BlockSpec((R, tn), lambda j, i: (0, j))],
        out_shape=[out_shape, jax.ShapeDtypeStruct((R, ncols), out_dtype)],
        scratch_shapes=[pltpu.VMEM((tn, K) if w_transposed else (K, tn), BF16)],
        compiler_params=_params("parallel", "arbitrary"),
        name=name,
    )(*args)


def _mm_fullk_kernel(a_ref, w_ref, r_ref, as_ref, rs_ref, o_ref, os_ref):
    wb = w_ref[...].astype(BF16)
    o_ref[...] = r_ref[...] + jnp.dot(a_ref[...], wb, preferred_element_type=F32)

    @pl.when(pl.program_id(0) == 0)
    def _():
        os_ref[...] = rs_ref[...] + jnp.dot(as_ref[...], wb, preferred_element_type=F32)

    @pl.when(pl.program_id(0) > 0)
    def _():
        os_ref[...] = jnp.zeros_like(os_ref)


def mm_fullk(a, a_s, w, layer, residual, residual_s, *, tm, tn, name="mm_fullk"):
    M, K = a.shape
    R = a_s.shape[0]
    N = w.shape[2]
    out, out_s = pl.pallas_call(
        _mm_fullk_kernel,
        grid=(M // tm, N // tn),
        in_specs=[pl.BlockSpec((tm, K), lambda i, j: (i, 0), pipeline_mode=pl.Buffered(1)),
                  pl.BlockSpec((None, K, tn), lambda i, j: (layer, 0, j)),
                  pl.BlockSpec((tm, tn), lambda i, j: (i, j)),
                  pl.BlockSpec((R, K), lambda i, j: (0, 0)),
                  pl.BlockSpec((R, tn), lambda i, j: (0, j))],
        out_specs=[pl.BlockSpec((tm, tn), lambda i, j: (i, j)),
                   pl.BlockSpec((None, R, tn), lambda i, j: (i, 0, j))],
        out_shape=[jax.ShapeDtypeStruct((M, N), F32), jax.ShapeDtypeStruct((M // tm, R, N), F32)],
        compiler_params=_params("arbitrary", "arbitrary"),
        name=name,
    )(a, w, residual, a_s, residual_s)
    return out, out_s[0]


def _silu(x):
    return x * jax.nn.sigmoid(x)


def _ffn_in_kernel(a_ref, wg_ref, wu_ref, cw_ref, cb_ref, as_ref, s0_ref, s1_ref,
                   act_ref, st_ref, acts_ref, gs_ref, wgb_ref, wub_ref, carry_ref, *, tiles_per_seq):
    i = pl.program_id(1)
    cw = cw_ref[...]
    cb = cb_ref[...]

    @pl.when(i == 0)
    def _():
        wgb_ref[...] = wg_ref[...].astype(BF16)
        wub_ref[...] = wu_ref[...].astype(BF16)
        a_s = as_ref[...]
        g_s = jnp.dot(a_s, wgb_ref[...], preferred_element_type=F32)
        u_s = jnp.dot(a_s, wub_ref[...], preferred_element_type=F32)
        c_s = cb + cw[0:1] * s0_ref[...] + cw[1:2] * s1_ref[...] + cw[2:3] * g_s
        acts_ref[...] = (_silu(c_s) * u_s).astype(acts_ref.dtype)
        gs_ref[...] = g_s

    @pl.when(i % tiles_per_seq == 0)
    def _():
        carry_ref[...] = jnp.zeros_like(carry_ref)

    a = a_ref[...]
    g = jnp.dot(a, wgb_ref[...], preferred_element_type=F32)
    u = jnp.dot(a, wub_ref[...], preferred_element_type=F32)
    tm = g.shape[0]
    prev = carry_ref[...]
    row = lax.broadcasted_iota(jnp.int32, g.shape, 0)
    g1 = jnp.where(row == 0, prev[7:8], pltpu.roll(g, 1, axis=0))
    g2 = jnp.where(row == 0, prev[6:7],
                   jnp.where(row == 1, prev[7:8], pltpu.roll(g, 2, axis=0)))
    c = cb + cw[0:1] * g2 + cw[1:2] * g1 + cw[2:3] * g
    act_ref[...] = (_silu(c) * u).astype(act_ref.dtype)
    carry_ref[...] = g[tm - 8:tm]
    st_ref[...] = g[tm - (CONV_W - 1):tm]


def ffn_in(a, a_s, w_in, conv_w, conv_b, st0, st1, layer, *, seq, tm, tw):
    M, K = a.shape
    R = a_s.shape[0]
    d_ff = w_in.shape[2] // 2
    nb = M // seq
    tps = seq // tm
    ub0 = d_ff // tw
    kern = functools.partial(_ffn_in_kernel, tiles_per_seq=tps)
    return pl.pallas_call(
        kern,
        grid=(d_ff // tw, M // tm),
        in_specs=[pl.BlockSpec((tm, K), lambda j, i: (i, 0)),
                  pl.BlockSpec((None, K, tw), lambda j, i: (layer, 0, j)),
                  pl.BlockSpec((None, K, tw), lambda j, i: (layer, 0, ub0 + j)),
                  pl.BlockSpec((None, CONV_W, tw), lambda j, i: (layer, 0, j)),
                  pl.BlockSpec((None, 1, tw), lambda j, i: (layer, 0, j)),
                  pl.BlockSpec((R, K), lambda j, i: (0, 0)),
                  pl.BlockSpec((R, tw), lambda j, i: (0, j)),
                  pl.BlockSpec((R, tw), lambda j, i: (0, j))],
        out_specs=[pl.BlockSpec((tm, tw), lambda j, i: (i, j)),
                   pl.BlockSpec((None, CONV_W - 1, tw), lambda j, i: (i // tps, 0, j)),
                   pl.BlockSpec((R, tw), lambda j, i: (0, j)),
                   pl.BlockSpec((R, tw), lambda j, i: (0, j))],
        out_shape=[jax.ShapeDtypeStruct((M, d_ff), BF16),
                   jax.ShapeDtypeStruct((nb, CONV_W - 1, d_ff), F32),
                   jax.ShapeDtypeStruct((R, d_ff), BF16),
                   jax.ShapeDtypeStruct((R, d_ff), F32)],
        scratch_shapes=[pltpu.VMEM((K, tw), BF16), pltpu.VMEM((K, tw), BF16),
                        pltpu.VMEM((8, tw), F32)],
        compiler_params=_params("parallel", "arbitrary"),
        name="ffn_in",
    )(a, w_in, w_in, conv_w, conv_b, a_s, st0, st1)


def _gla_tables(C):
    n_levels = int(math.log2(C))
    assert 1 << n_levels == C
    i = np.arange(C)[:, None]
    t = np.arange(C)[None, :]
    mats = [t <= i]
    level = np.full((C, C), -1, np.int32)
    level[np.arange(C), np.arange(C)] = n_levels
    for lv in range(n_levels):
        s = 1 << lv
        mid = (i // (2 * s)) * (2 * s) + s
        upper = i >= mid
        if 2 * s <= SUBLANES:
            mats.append((upper & (t >= mid) & (t <= i)) | (~upper & (t > i) & (t < mid)))
        same = (i // (2 * s)) == (t // (2 * s))
        own = same & upper & ~((t % (2 * s)) >= s)
        level[own] = lv
    return np.concatenate(mats, axis=0).astype(np.float32), level, n_levels


def _nt(a, b):
    return lax.dot_general(a, b, (((1,), (1,)), ((), ())), preferred_element_type=F32)


def _tn(a, b):
    return lax.dot_general(a, b, (((0,), (0,)), ((), ())), preferred_element_type=F32)


def _gla_kernel(q_ref, k_ref, v_ref, r_ref, gl_ref, w2_ref, bg_ref, ng_ref, s0_ref, ms_ref, lvl_ref,
                o_ref, sout_ref, st_ref, *, C, n_levels, seq_len, dk, dv, heads):
    c = pl.program_id(2)
    nc = pl.num_programs(2)

    @pl.when(c == 0)
    def _():
        for hh in range(heads):
            st_ref[hh] = s0_ref[hh].T

    glb = gl_ref[...].astype(BF16)
    ms = ms_ref[...]
    lvl = lvl_ref[...]
    for hh in range(heads):
        ks = slice(hh * dk, (hh + 1) * dk)
        vs = slice(hh * dv, (hh + 1) * dv)
        z = jnp.dot(glb, w2_ref[:, ks].astype(BF16), preferred_element_type=F32) + bg_ref[:, ks]
        g = -(jnp.maximum(-z, 0.0) + jnp.log1p(jnp.exp(-jnp.abs(z)))) / GLA_GATE_TAU
        if seq_len % C:
            row = c * C + lax.broadcasted_iota(jnp.int32, g.shape, 0)
            g = jnp.where(row < seq_len, g, 0.0)
        g_hi = g.astype(BF16)
        g_lo = (g - g_hi.astype(F32)).astype(BF16)
        ex = jnp.dot(ms, g_hi, preferred_element_type=F32) + jnp.dot(ms, g_lo, preferred_element_type=F32)
        b = ex[0:C]
        e_b = jnp.exp(b)
        e_end = jnp.exp(b[C - 1:C] - b)

        q = q_ref[:, ks] * (dk ** -0.5)
        k = k_ref[:, ks]
        vb = v_ref[:, vs].astype(BF16)
        st = st_ref[hh]
        o = _nt((q * e_b).astype(BF16), st.astype(BF16))
        a = jnp.where(lvl == n_levels, _nt(q.astype(BF16), k.astype(BF16)), 0.0)
        for lv in range(n_levels):
            s = 1 << lv
            if 2 * s <= SUBLANES:
                e_l = jnp.exp(ex[(1 + lv) * C:(2 + lv) * C])
            else:
                m = jnp.broadcast_to(b.reshape(C // (2 * s), 2 * s, dk)[:, s - 1:s, :], (C // (2 * s), 2 * s, dk))
                e_l = jnp.exp(-jnp.abs(b - m.reshape(C, dk)))
            a = jnp.where(lvl == lv, _nt((q * e_l).astype(BF16), (k * e_l).astype(BF16)), a)
        o = o + jnp.dot(a.astype(BF16), vb, preferred_element_type=F32)
        decay_end = e_b[C - 1:C]
        st_ref[hh] = st * decay_end + _tn(vb, (k * e_end).astype(BF16))

        y = o * lax.rsqrt(jnp.mean(o * o, axis=-1, keepdims=True) + EPS) * ng_ref[...]
        o_ref[:, vs] = (y * _silu(r_ref[:, vs])).astype(o_ref.dtype)

    @pl.when(c == nc - 1)
    def _():
        for hh in range(heads):
            sout_ref[hh] = st_ref[hh].T


def gla_core(h, gl, w_gate2, b_gate, norm_gain, s0, layer, s0_layer, *, seq_len, C, n_heads, dk, dv,
             heads_per_step):
    B, Lp, _ = h.shape
    ms_np, lvl_np, n_levels = _gla_tables(C)
    ms = jnp.asarray(ms_np, BF16)
    lvl = jnp.asarray(lvl_np)
    hps = heads_per_step
    assert n_heads % hps == 0
    ng = n_heads // hps
    v0 = 2 * n_heads * dk // (hps * dv)
    r0 = v0 + ng
    kern = functools.partial(_gla_kernel, C=C, n_levels=n_levels, seq_len=seq_len, dk=dk, dv=dv, heads=hps)
    R = ms_np.shape[0]
    return pl.pallas_call(
        kern,
        grid=(B, ng, Lp // C),
        in_specs=[pl.BlockSpec((None, C, hps * dk), lambda b, hd, c: (b, c, hd)),
                  pl.BlockSpec((None, C, hps * dk), lambda b, hd, c: (b, c, ng + hd)),
                  pl.BlockSpec((None, C, hps * dv), lambda b, hd, c: (b, c, v0 + hd)),
                  pl.BlockSpec((None, C, hps * dv), lambda b, hd, c: (b, c, r0 + hd)),
                  pl.BlockSpec((None, C, LANES), lambda b, hd, c: (b, c, 0)),
                  pl.BlockSpec((None, LANES, hps * dk), lambda b, hd, c: (layer, 0, hd)),
                  pl.BlockSpec((None, 1, hps * dk), lambda b, hd, c: (layer, 0, hd)),
                  pl.BlockSpec((None, 1, dv), lambda b, hd, c: (layer, 0, 0)),
                  pl.BlockSpec((None, None, hps, dk, dv), lambda b, hd, c: (s0_layer, b, hd, 0, 0)),
                  pl.BlockSpec((R, C), lambda b, hd, c: (0, 0)),
                  pl.BlockSpec((C, C), lambda b, hd, c: (0, 0))],
        out_specs=[pl.BlockSpec((None, C, hps * dv), lambda b, hd, c: (b, c, hd)),
                   pl.BlockSpec((None, hps, dk, dv), lambda b, hd, c: (b, hd, 0, 0))],
        out_shape=[jax.ShapeDtypeStruct((B, Lp, n_heads * dv), BF16),
                   jax.ShapeDtypeStruct((B, n_heads, dk, dv), F32)],
        scratch_shapes=[pltpu.VMEM((hps, dv, dk), F32)],
        compiler_params=_params("parallel", "parallel", "arbitrary"),
        name="gla_core",
    )(h, h, h, h, gl, w_gate2, b_gate, norm_gain, s0, ms, lvl)


def _rope_tables(pos):
    half = ROT_DIM // 2
    inv = jnp.power(ROPE_THETA, -jnp.arange(half, dtype=F32) * (2.0 / ROT_DIM))
    ang = pos.astype(F32)[:, None] * inv[None, :]
    cos, sin = jnp.cos(ang), jnp.sin(ang)
    n = pos.shape[0]
    cos_t = jnp.concatenate([cos, cos, jnp.ones((n, HEAD_DIM - ROT_DIM), F32)], axis=1)
    sin_t = jnp.concatenate([-sin, sin, jnp.zeros((n, HEAD_DIM - ROT_DIM), F32)], axis=1)
    return cos_t, sin_t


def _qkv_epilogue(product, o_ref, gain_ref, cos_ref, sin_ref, *, tn, section, head_major, row_chunk):
    j = pl.program_id(0)
    kind = (j * tn // section) % 3
    tm = cos_ref.shape[0]
    chunks = [slice(r, r + row_chunk) for r in range(0, tm, row_chunk)]

    def put(rows, hh, val):
        if head_major:
            o_ref[hh, rows, :] = val
        else:
            o_ref[rows, hh * HEAD_DIM:(hh + 1) * HEAD_DIM] = val

    @pl.when(kind == 2)
    def _():
        for rows in chunks:
            acc = product(rows)
            for hh in range(tn // HEAD_DIM):
                put(rows, hh, acc[:, hh * HEAD_DIM:(hh + 1) * HEAD_DIM])

    @pl.when(kind != 2)
    def _():
        half = ROT_DIM // 2
        gain = gain_ref[...]
        lane = lax.broadcasted_iota(jnp.int32, (row_chunk, HEAD_DIM), 1)
        for rows in chunks:
            acc = product(rows)
            cos_t = cos_ref[rows, :]
            sin_t = sin_ref[rows, :]
            for hh in range(tn // HEAD_DIM):
                x = acc[:, hh * HEAD_DIM:(hh + 1) * HEAD_DIM]
                y = x * lax.rsqrt(jnp.mean(x * x, axis=-1, keepdims=True) + EPS) * gain
                swapped = jnp.where(lane < half, pltpu.roll(y, HEAD_DIM - half, axis=1),
                                    pltpu.roll(y, half, axis=1))
                put(rows, hh, y * cos_t + swapped * sin_t)


def dsa_qkv(a, a_s, w_qkv, gains, rope, rope_s, layer, *, tm, tn, section, seq):
    M = a.shape[0]
    R = a_s.shape[0]
    tps = seq // tm
    n_sec = w_qkv.shape[2] // section
    gain_spec = pl.BlockSpec((None, 1, HEAD_DIM), lambda j, i: (layer * n_sec + j * tn // section, 0, 0))
    extra_specs = [gain_spec,
                   pl.BlockSpec((tm, HEAD_DIM), lambda j, i: (i % tps, 0)),
                   pl.BlockSpec((tm, HEAD_DIM), lambda j, i: (i % tps, 0))]
    extra_specs_s = [gain_spec,
                     pl.BlockSpec((R, HEAD_DIM), lambda j, i: (0, 0)),
                     pl.BlockSpec((R, HEAD_DIM), lambda j, i: (0, 0))]
    epi = functools.partial(_qkv_epilogue, tn=tn, section=section, head_major=True, row_chunk=min(256, tm))
    epi_s = functools.partial(_qkv_epilogue, tn=tn, section=section, head_major=False, row_chunk=R)
    hpb = tn // HEAD_DIM
    out_spec = pl.BlockSpec((None, hpb, tm, HEAD_DIM), lambda j, i: (i // tps, j, i % tps, 0))
    out_shape = jax.ShapeDtypeStruct((M // seq, w_qkv.shape[2] // HEAD_DIM, seq, HEAD_DIM), F32)
    return mm_wres(a, a_s, w_qkv, layer, tm=tm, tn=tn, epilogue=epi, epilogue_s=epi_s,
                   extra=(gains, *rope), extra_specs=extra_specs,
                   extra_s=(gains, *rope_s), extra_specs_s=extra_specs_s,
                   out_spec=out_spec, out_shape=out_shape, name="dsa_qkv")


def _rows(start, size, stride):
    if stride == 1:
        return (pl.ds(start, size), slice(None))
    return (pl.ds(start, size, stride=stride), slice(None))


def _prompt_attn_kernel(*refs, seq, merge_rows):
    qkv_refs = refs[:3 * N_GROUPS]
    out_ref = refs[3 * N_GROUPS]
    o_scr = refs[3 * N_GROUPS + 1:3 * N_GROUPS + 1 + N_GROUPS]
    l_scr = refs[3 * N_GROUPS + 1 + N_GROUPS:]
    scale = HEAD_DIM ** -0.5
    for g, (window, dil) in enumerate(DSA_GROUPS):
        q_ref, k_ref, v_ref = qkv_refs[3 * g:3 * g + 3]
        blk = window // dil
        span = blk * dil
        nb = seq // span
        qi = lax.broadcasted_iota(jnp.int32, (blk, 2 * blk), 0)
        ki = lax.broadcasted_iota(jnp.int32, (blk, 2 * blk), 1)
        both_ok = (ki >= qi) & (ki <= qi + blk)
        first_ok = (lax.broadcasted_iota(jnp.int32, (blk, blk), 1)
                    <= lax.broadcasted_iota(jnp.int32, (blk, blk), 0))
        for r in range(dil):
            for n in range(nb):
                start = r + n * span
                q = q_ref[_rows(start, blk, dil)].astype(BF16)
                if n == 0:
                    kk = k_ref[_rows(start, blk, dil)].astype(BF16)
                    vv = v_ref[_rows(start, blk, dil)].astype(BF16)
                    ok = first_ok
                else:
                    kk = k_ref[_rows(start - span, 2 * blk, dil)].astype(BF16)
                    vv = v_ref[_rows(start - span, 2 * blk, dil)].astype(BF16)
                    ok = both_ok
                s = jnp.where(ok, _nt(q, kk) * scale, -jnp.inf)
                mx = jnp.max(s, axis=-1, keepdims=True)
                p = jnp.exp(s - mx)
                den = jnp.sum(p, axis=-1, keepdims=True)
                o = jnp.dot(p.astype(BF16), vv, preferred_element_type=F32) / den
                o_scr[g][_rows(start, blk, dil)] = o
                l_scr[g][_rows(start, blk, dil)] = jnp.broadcast_to(mx + jnp.log(den), (blk, HEAD_DIM))
    for c in range(seq // merge_rows):
        sl = slice(c * merge_rows, (c + 1) * merge_rows)
        ls = [l_scr[g][sl, :] for g in range(N_GROUPS)]
        mx = functools.reduce(jnp.maximum, ls)
        es = [jnp.exp(l - mx) for l in ls]
        den = functools.reduce(lambda a, b: a + b, es)
        acc = es[0] * o_scr[0][sl, :]
        for g in range(1, N_GROUPS):
            acc = acc + es[g] * o_scr[g][sl, :]
        out_ref[sl, :] = (acc / den).astype(out_ref.dtype)


def prompt_attention(qkvh, *, n_heads):
    B, _, S, _ = qkvh.shape
    for window, dil in DSA_GROUPS:
        assert S % window == 0
    in_specs = []
    for g in range(N_GROUPS):
        for t in range(3):
            in_specs.append(pl.BlockSpec((None, None, S, HEAD_DIM),
                                         lambda b, h, g=g, t=t: (b, (g * 3 + t) * n_heads + h, 0, 0)))
    kern = functools.partial(_prompt_attn_kernel, seq=S, merge_rows=min(256, S))
    return pl.pallas_call(
        kern,
        grid=(B, n_heads),
        in_specs=in_specs,
        out_specs=pl.BlockSpec((None, S, HEAD_DIM), lambda b, h: (b, 0, h)),
        out_shape=jax.ShapeDtypeStruct((B, S, n_heads * HEAD_DIM), BF16),
        scratch_shapes=[pltpu.VMEM((S, HEAD_DIM), F32)] * (2 * N_GROUPS),
        compiler_params=_params("parallel", "parallel"),
        name="prompt_attention",
    )(*([qkvh] * (3 * N_GROUPS)))


def _kv_export_kernel(*refs, n_layers, n_heads, ts):
    out_ref = refs[2 * n_layers]
    layer = pl.program_id(0)
    for l in range(n_layers):
        k_ref, v_ref = refs[2 * l], refs[2 * l + 1]

        @pl.when(layer == l)
        def _():
            kt = jnp.swapaxes(k_ref[...], 0, 1)
            vt = jnp.swapaxes(v_ref[...], 0, 1)
            out_ref[...] = jnp.concatenate([kt, vt], axis=1).reshape(ts * 2 * n_heads, HEAD_DIM)


def kv_export(qkvh_layers, group, *, keep, n_heads, ts):
    n_layers = len(qkvh_layers)
    B, _, S, _ = qkvh_layers[0].shape
    nt = keep // ts
    t0 = (S - keep) // ts
    in_specs, args = [], []
    for l, arr in enumerate(qkvh_layers):
        for t in (1, 2):
            def idx(ll, b, i, l=l, t=t):
                bb = jnp.where(ll == l, b, jnp.where(ll < l, 0, B - 1))
                ii = jnp.where(ll == l, i, jnp.where(ll < l, 0, nt - 1))
                return (bb, group * 3 + t, t0 + ii, 0)
            in_specs.append(pl.BlockSpec((None, n_heads, ts, HEAD_DIM), idx))
            args.append(arr)
    rows = 2 * n_heads
    out = pl.pallas_call(
        functools.partial(_kv_export_kernel, n_layers=n_layers, n_heads=n_heads, ts=ts),
        grid=(n_layers, B, nt),
        in_specs=in_specs,
        out_specs=pl.BlockSpec((None, None, ts * rows, HEAD_DIM), lambda ll, b, i: (ll, b, i, 0)),
        out_shape=jax.ShapeDtypeStruct((n_layers, B, keep * rows, HEAD_DIM), F32),
        compiler_params=_params("arbitrary", "arbitrary", "arbitrary"),
        name=f"kv_export_g{group}",
    )(*args)
    return out.reshape(n_layers, B, keep, 2, n_heads, HEAD_DIM)


def _sample_attn_kernel(qkv_ref, c0_ref, c1_ref, c2_ref, out_ref, *, n_heads):
    scale = HEAD_DIM ** -0.5
    os, ls = [], []
    for g, c_ref in enumerate((c0_ref, c1_ref, c2_ref)):
        q, k_new, v_new = qkv_ref[3 * g], qkv_ref[3 * g + 1], qkv_ref[3 * g + 2]
        kc = c_ref[:, 0:n_heads, :]
        vc = c_ref[:, n_heads:2 * n_heads, :]
        s_c = jnp.sum(kc * q[None], axis=-1, keepdims=True) * scale
        s_n = jnp.sum(k_new * q, axis=-1, keepdims=True) * scale
        mx = jnp.maximum(jnp.max(s_c, axis=0), s_n)
        p_c = jnp.exp(s_c - mx[None])
        p_n = jnp.exp(s_n - mx)
        den = jnp.sum(p_c, axis=0) + p_n
        os.append((jnp.sum(p_c * vc, axis=0) + p_n * v_new) / den)
        ls.append(mx + jnp.log(den))
    mx = functools.reduce(jnp.maximum, ls)
    es = [jnp.exp(l - mx) for l in ls]
    den = functools.reduce(lambda a, b: a + b, es)
    acc = es[0] * os[0]
    for g in range(1, N_GROUPS):
        acc = acc + es[g] * os[g]
    out_ref[...] = (acc / den).astype(out_ref.dtype)


def sample_attention(qkv, caches, layer, *, n_heads):
    B = qkv.shape[0]
    rows = 2 * n_heads
    views, specs = [], []
    for (window, dil), c in zip(DSA_GROUPS, caches):
        n_buf = c.shape[2]
        assert n_buf == window and n_buf % dil == 0
        views.append(c.reshape(c.shape[0], B, n_buf // dil, dil * rows, HEAD_DIM))
        specs.append(pl.BlockSpec((None, None, n_buf // dil, rows, HEAD_DIM), lambda b: (layer, b, 0, 0, 0)))
    return pl.pallas_call(
        functools.partial(_sample_attn_kernel, n_heads=n_heads),
        grid=(B,),
        in_specs=[pl.BlockSpec((None, 3 * N_GROUPS, n_heads, HEAD_DIM), lambda b: (b, 0, 0, 0))] + specs,
        out_specs=pl.BlockSpec((None, n_heads, HEAD_DIM), lambda b: (b, 0, 0)),
        out_shape=jax.ShapeDtypeStruct((B, n_heads, HEAD_DIM), F32),
        compiler_params=_params("parallel"),
        name="sample_attention",
    )(qkv, *views)


def _set_last_kernel(rolled_ref, new_ref, out_ref):
    del rolled_ref
    out_ref[...] = new_ref[...]


def cache_shift(cache, new):
    NL, B, n_buf = cache.shape[:3]
    rows = cache.shape[3] * cache.shape[4]
    LB = NL * B
    c4 = cache.reshape(LB, n_buf, rows, HEAD_DIM)
    n4 = new.reshape(LB, 1, rows, HEAD_DIM)
    mesh = plsc.VectorSubcoreMesh(core_axis_name="core", subcore_axis_name="subcore")
    n_workers = mesh.num_cores * mesh.num_subcores
    n_moves = LB * (n_buf - 1)

    @pl.kernel(out_type=jax.ShapeDtypeStruct(c4.shape, c4.dtype), mesh=mesh,
               scratch_types=[pltpu.VMEM((rows, HEAD_DIM), c4.dtype)])
    def roll(x_hbm, o_hbm, buf):
        worker = lax.axis_index("core") * mesh.num_subcores + lax.axis_index("subcore")

        @pl.loop(0, pl.cdiv(n_moves, n_workers))
        def _(it):
            q = it * n_workers + worker

            @pl.when(q < n_moves)
            def _():
                lb = q // (n_buf - 1)
                p = q % (n_buf - 1)
                pltpu.sync_copy(x_hbm.at[lb, p + 1], buf)
                pltpu.sync_copy(buf, o_hbm.at[lb, p])

    rolled = roll(c4)
    out = pl.pallas_call(
        _set_last_kernel,
        grid=(LB,),
        in_specs=[pl.BlockSpec((None, 1, rows, HEAD_DIM), lambda lb: (lb, n_buf - 1, 0, 0)),
                  pl.BlockSpec((None, 1, rows, HEAD_DIM), lambda lb: (lb, 0, 0, 0))],
        out_specs=pl.BlockSpec((None, 1, rows, HEAD_DIM), lambda lb: (lb, n_buf - 1, 0, 0)),
        out_shape=jax.ShapeDtypeStruct(c4.shape, c4.dtype),
        input_output_aliases={0: 0},
        compiler_params=_params("parallel"),
        name="cache_set_last",
    )(rolled, n4)
    return out.reshape(cache.shape)


def kernel(x_prompt, x_sample, cache_kv_w128, cache_kv_w512, cache_kv_w2048, state_gla, state_ffn_conv,
           norm_mix, norm_ffn, dsa_w_qkv, dsa_q_gain, dsa_k_gain, dsa_w_o,
           gla_w_in, gla_w_gate2, gla_b_gate, gla_norm_gain, gla_w_o,
           ffn_w_in, ffn_conv_w, ffn_conv_b, ffn_w_out):
    B, S, D = x_prompt.shape
    BS = x_sample.shape[0]
    assert x_sample.shape[1] == 1
    depth = norm_mix.shape[0]
    caches = (cache_kv_w128, cache_kv_w512, cache_kv_w2048)
    n_gla, gla_heads, gla_dk, gla_dv = state_gla.shape[0], state_gla.shape[2], state_gla.shape[3], state_gla.shape[4]
    gla_kd, gla_vd = gla_heads * gla_dk, gla_heads * gla_dv
    dsa_heads = dsa_w_o.shape[1] // HEAD_DIM
    W = dsa_heads * HEAD_DIM
    d_ff = ffn_w_in.shape[2] // 2
    M = B * S
    R = SAMPLE_ROWS

    tm = min(1024, S)
    tn = min(512, W)
    tw = 256
    tm_rms = min(512, M)
    gla_chunk = min(256, S)
    gla_hps = min(4, gla_heads)

    xp = x_prompt.reshape(M, D)
    xs = jnp.pad(x_sample.reshape(BS, D), ((0, R - BS), (0, 0)))

    norm_mix3 = norm_mix.reshape(depth, 1, D)
    norm_ffn3 = norm_ffn.reshape(depth, 1, D)
    conv_b3 = ffn_conv_b.reshape(depth, 1, d_ff)
    n_main = 2 * gla_kd + 2 * gla_vd
    w_gl = jnp.pad(gla_w_in[:, :, n_main:], ((0, 0), (0, 0), (0, LANES - GLA_GATE_RANK)))
    gla_w_in_t = jnp.swapaxes(gla_w_in, 1, 2)
    w_gate2 = jnp.pad(gla_w_gate2, ((0, 0), (0, LANES - GLA_GATE_RANK), (0, 0)))
    b_gate3 = gla_b_gate.reshape(n_gla, 1, gla_kd)
    gla_ng3 = gla_norm_gain.reshape(n_gla, 1, gla_dv)
    ones = jnp.ones_like(dsa_q_gain)
    qk_gains = jnp.stack([dsa_q_gain, dsa_k_gain, ones], axis=2).reshape(-1, 1, HEAD_DIM)
    rope_p = _rope_tables(jnp.arange(S))
    rope_s = _rope_tables(jnp.full((R,), PAST_LEN, jnp.int32))

    qkvh_layers = []
    kv_new = [[] for _ in DSA_GROUPS]
    gla_p, gla_s, conv_p, conv_s = [], [], [], []
    zero_state = jnp.zeros((1, B, gla_heads, gla_dk, gla_dv), F32)

    for i in range(depth):
        j = i // 2
        hp = rms_bf16(xp, norm_mix3, i, tm_rms)
        hs = rms_bf16(xs, norm_mix3, i, R)
        if i % 2 == 0:
            h, h_s = mm_wres(hp, hs, gla_w_in_t, j, tm=tm, tn=tn, ncols=n_main, w_transposed=True, name="gla_in")
            gl, gl_s = mm_wres(hp, hs, w_gl, j, tm=tm, tn=LANES, name="gla_gate_in")
            o, sp = gla_core(h.reshape(B, S, n_main), gl.reshape(B, S, LANES), w_gate2, b_gate3, gla_ng3,
                             zero_state, j, 0, seq_len=S, C=gla_chunk, n_heads=gla_heads, dk=gla_dk, dv=gla_dv,
                             heads_per_step=gla_hps)
            h_s = jnp.pad(h_s[:BS].reshape(BS, 1, n_main), ((0, 0), (0, R - 1), (0, 0)))
            gl_s = jnp.pad(gl_s[:BS].reshape(BS, 1, LANES), ((0, 0), (0, R - 1), (0, 0)))
            o_s, ss = gla_core(h_s, gl_s, w_gate2, b_gate3, gla_ng3, state_gla, j, j,
                               seq_len=1, C=R, n_heads=gla_heads, dk=gla_dk, dv=gla_dv, heads_per_step=gla_hps)
            o_s = jnp.pad(o_s[:, 0], ((0, R - BS), (0, 0)))
            xp, xs = mm_wres(o.reshape(M, gla_vd), o_s, gla_w_o, j, tm=tm, tn=tn, residual=xp, residual_s=xs,
                             name="gla_out")
            gla_p.append(sp)
            gla_s.append(ss)
        else:
            qkvh, qkv_s = dsa_qkv(hp, hs, dsa_w_qkv, qk_gains, rope_p, rope_s, j, tm=tm, tn=tn, section=W, seq=S)
            merged = prompt_attention(qkvh, n_heads=dsa_heads)
            qkvh_layers.append(qkvh)
            qkv_s = qkv_s[:BS].reshape(BS, N_GROUPS, 3, dsa_heads, HEAD_DIM)
            merged_s = sample_attention(qkv_s.reshape(BS, 3 * N_GROUPS, dsa_heads, HEAD_DIM), caches, j,
                                        n_heads=dsa_heads)
            merged_s = jnp.pad(merged_s.reshape(BS, W).astype(BF16), ((0, R - BS), (0, 0)))
            xp, xs = mm_wres(merged.reshape(M, W), merged_s, dsa_w_o, j, tm=tm, tn=tn, residual=xp,
                             residual_s=xs, name="dsa_out")
            for g in range(N_GROUPS):
                kv_new[g].append(qkv_s[:, g, 1:3])
        fp = rms_bf16(xp, norm_ffn3, i, tm_rms)
        fs = rms_bf16(xs, norm_ffn3, i, R)
        st = jnp.pad(state_ffn_conv[i], ((0, R - BS), (0, 0), (0, 0)))
        act, cp, act_s, g_s = ffn_in(fp, fs, ffn_w_in, ffn_conv_w, conv_b3, st[:, 0], st[:, 1], i,
                                     seq=S, tm=tm, tw=tw)
        xp, xs = mm_fullk(act, act_s, ffn_w_out, i, xp, xs, tm=tm, tn=tw, name="ffn_out")
        conv_p.append(cp)
        conv_s.append(jnp.stack([state_ffn_conv[i][:, 1], g_s[:BS]], axis=1))

    kv128_p, kv512_p, kv2048_p = [
        kv_export(qkvh_layers, g, keep=min(window, S), n_heads=dsa_heads, ts=min(256, window, S))
        for g, (window, _) in enumerate(DSA_GROUPS)]
    kv128_s, kv512_s, kv2048_s = [cache_shift(c, jnp.stack(n, axis=0))
                                  for c, n in zip(caches, kv_new)]
    return (xp.reshape(B, S, D), xs[:BS].reshape(BS, 1, D),
            kv128_p, kv128_s, kv512_p, kv512_s, kv2048_p, kv2048_s,
            jnp.stack(gla_p, axis=0), jnp.stack(gla_s, axis=0),
            jnp.stack(conv_p, axis=0), jnp.stack(conv_s, axis=0))
```

```python
import functools
import math

import numpy as np
import jax
import jax.numpy as jnp
from jax import lax
from jax.experimental import pallas as pl
from jax.experimental.pallas import tpu as pltpu
from jax.experimental.pallas import tpu_sc as plsc

F32 = jnp.float32
BF16 = jnp.bfloat16

EPS = 1e-6
HEAD_DIM = 128
ROT_DIM = HEAD_DIM // 4
ROPE_THETA = 500000.0
DSA_GROUPS = ((128, 1), (512, 4), (2048, 16))
N_GROUPS = len(DSA_GROUPS)
GLA_GATE_RANK = 16
GLA_GATE_TAU = 16.0
CONV_W = 3
PAST_LEN = 8192
LANES = 128
SUBLANES = 8
SAMPLE_ROWS = 16
VMEM_LIMIT_BYTES = 56 * 1024 * 1024


def _params(*sem):
    return pltpu.CompilerParams(dimension_semantics=sem, vmem_limit_bytes=VMEM_LIMIT_BYTES)


def _rms_kernel(x_ref, g_ref, o_ref):
    x = x_ref[...]
    y = x * lax.rsqrt(jnp.mean(x * x, axis=-1, keepdims=True) + EPS)
    o_ref[...] = (y * g_ref[...]).astype(o_ref.dtype)


def rms_bf16(x, gains, layer, tm):
    M, D = x.shape
    return pl.pallas_call(
        _rms_kernel,
        grid=(M // tm,),
        in_specs=[pl.BlockSpec((tm, D), lambda i: (i, 0)),
                  pl.BlockSpec((None, 1, D), lambda i: (layer, 0, 0))],
        out_specs=pl.BlockSpec((tm, D), lambda i: (i, 0)),
        out_shape=jax.ShapeDtypeStruct((M, D), BF16),
        compiler_params=_params("parallel"),
        name="rms_bf16",
    )(x, gains)


def _mm_wres_kernel(*refs, n_extra, n_extra_s, has_res, epilogue, epilogue_s, w_transposed):
    it = iter(refs)
    a_ref, w_ref = next(it), next(it)
    extra = [next(it) for _ in range(n_extra)]
    r_ref = next(it) if has_res else None
    as_ref = next(it)
    extra_s = [next(it) for _ in range(n_extra_s)]
    rs_ref = next(it) if has_res else None
    o_ref, os_ref, wb_ref = next(it), next(it), next(it)
    i = pl.program_id(1)

    def product_of(lhs_ref, res_ref):
        def product(rows=slice(None)):
            if w_transposed:
                acc = _nt(lhs_ref[rows, :], wb_ref[...])
            else:
                acc = jnp.dot(lhs_ref[rows, :], wb_ref[...], preferred_element_type=F32)
            if has_res:
                acc = acc + res_ref[rows, :]
            return acc
        return product

    @pl.when(i == 0)
    def _():
        wb_ref[...] = w_ref[...].astype(BF16)
        if epilogue_s is None:
            os_ref[...] = product_of(as_ref, rs_ref)().astype(os_ref.dtype)
        else:
            epilogue_s(product_of(as_ref, rs_ref), os_ref, *extra_s)

    if epilogue is None:
        o_ref[...] = product_of(a_ref, r_ref)().astype(o_ref.dtype)
    else:
        epilogue(product_of(a_ref, r_ref), o_ref, *extra)


def mm_wres(a, a_s, w, layer, *, tm, tn, col0=0, ncols=None, out_dtype=F32, residual=None, residual_s=None,
            epilogue=None, epilogue_s=None, extra=(), extra_specs=(), extra_s=(), extra_specs_s=(),
            out_spec=None, out_shape=None, w_transposed=False, name="mm_wres"):
    M, K = a.shape
    R = a_s.shape[0]
    ncols = w.shape[1 if w_transposed else 2] - col0 if ncols is None else ncols
    cb0 = col0 // tn
    if w_transposed:
        w_spec = pl.BlockSpec((None, tn, K), lambda j, i: (layer, cb0 + j, 0))
    else:
        w_spec = pl.BlockSpec((None, K, tn), lambda j, i: (layer, 0, cb0 + j))
    if out_spec is None:
        out_spec = pl.BlockSpec((tm, tn), lambda j, i: (i, j))
        out_shape = jax.ShapeDtypeStruct((M, ncols), out_dtype)
    has_res = residual is not None
    in_specs = [pl.BlockSpec((tm, K), lambda j, i: (i, 0)), w_spec]
    in_specs += list(extra_specs)
    args = [a, w, *extra]
    if has_res:
        in_specs.append(pl.BlockSpec((tm, tn), lambda j, i: (i, j)))
        args.append(residual)
    in_specs.append(pl.BlockSpec((R, K), lambda j, i: (0, 0)))
    in_specs += list(extra_specs_s)
    args += [a_s, *extra_s]
    if has_res:
        in_specs.append(pl.BlockSpec((R, tn), lambda j, i: (0, j)))
        args.append(residual_s)
    kern = functools.partial(_mm_wres_kernel, n_extra=len(extra), n_extra_s=len(extra_s), has_res=has_res,
                             epilogue=epilogue, epilogue_s=epilogue_s, w_transposed=w_transposed)
    return pl.pallas_call(
        kern,
        grid=(ncols // tn, M // tm),
        in_specs=in_specs,
        out_specs=[out_spec, pl.BlockSpec((R, tn), lambda j, i: (0, j))],
        out_shape=[out_shape, jax.ShapeDtypeStruct((R, ncols), out_dtype)],
        scratch_shapes=[pltpu.VMEM((tn, K) if w_transposed else (K, tn), BF16)],
        compiler_params=_params("parallel", "arbitrary"),
        name=name,
    )(*args)


def _mm_fullk_kernel(a_ref, w_ref, r_ref, as_ref, rs_ref, o_ref, os_ref):
    wb = w_ref[...].astype(BF16)
    o_ref[...] = r_ref[...] + jnp.dot(a_ref[...], wb, preferred_element_type=F32)

    @pl.when(pl.program_id(0) == 0)
    def _():
        os_ref[...] = rs_ref[...] + jnp.dot(as_ref[...], wb, preferred_element_type=F32)

    @pl.when(pl.program_id(0) > 0)
    def _():
        os_ref[...] = jnp.zeros_like(os_ref)


def mm_fullk(a, a_s, w, layer, residual, residual_s, *, tm, tn, name="mm_fullk"):
    M, K = a.shape
    R = a_s.shape[0]
    N = w.shape[2]
    out, out_s = pl.pallas_call(
        _mm_fullk_kernel,
        grid=(M // tm, N // tn),
        in_specs=[pl.BlockSpec((tm, K), lambda i, j: (i, 0), pipeline_mode=pl.Buffered(1)),
                  pl.BlockSpec((None, K, tn), lambda i, j: (layer, 0, j)),
                  pl.BlockSpec((tm, tn), lambda i, j: (i, j)),
                  pl.BlockSpec((R, K), lambda i, j: (0, 0)),
                  pl.BlockSpec((R, tn), lambda i, j: (0, j))],
        out_specs=[pl.BlockSpec((tm, tn), lambda i, j: (i, j)),
                   pl.BlockSpec((None, R, tn), lambda i, j: (i, 0, j))],
        out_shape=[jax.ShapeDtypeStruct((M, N), F32), jax.ShapeDtypeStruct((M // tm, R, N), F32)],
        compiler_params=_params("arbitrary", "arbitrary"),
        name=name,
    )(a, w, residual, a_s, residual_s)
    return out, out_s[0]


def _silu(x):
    return x * jax.nn.sigmoid(x)


def _ffn_in_kernel(a_ref, wg_ref, wu_ref, cw_ref, cb_ref, as_ref, s0_ref, s1_ref,
                   act_ref, st_ref, acts_ref, gs_ref, wgb_ref, wub_ref, carry_ref, *, tiles_per_seq):
    i = pl.program_id(1)
    cw = cw_ref[...]
    cb = cb_ref[...]

    @pl.when(i == 0)
    def _():
        wgb_ref[...] = wg_ref[...].astype(BF16)
        wub_ref[...] = wu_ref[...].astype(BF16)
        a_s = as_ref[...]
        g_s = jnp.dot(a_s, wgb_ref[...], preferred_element_type=F32)
        u_s = jnp.dot(a_s, wub_ref[...], preferred_element_type=F32)
        c_s = cb + cw[0:1] * s0_ref[...] + cw[1:2] * s1_ref[...] + cw[2:3] * g_s
        acts_ref[...] = (_silu(c_s) * u_s).astype(acts_ref.dtype)
        gs_ref[...] = g_s

    @pl.when(i % tiles_per_seq == 0)
    def _():
        carry_ref[...] = jnp.zeros_like(carry_ref)

    a = a_ref[...]
    g = jnp.dot(a, wgb_ref[...], preferred_element_type=F32)
    u = jnp.dot(a, wub_ref[...], preferred_element_type=F32)
    tm = g.shape[0]
    prev = carry_ref[...]
    row = lax.broadcasted_iota(jnp.int32, g.shape, 0)
    g1 = jnp.where(row == 0, prev[7:8], pltpu.roll(g, 1, axis=0))
    g2 = jnp.where(row == 0, prev[6:7],
                   jnp.where(row == 1, prev[7:8], pltpu.roll(g, 2, axis=0)))
    c = cb + cw[0:1] * g2 + cw[1:2] * g1 + cw[2:3] * g
    act_ref[...] = (_silu(c) * u).astype(act_ref.dtype)
    carry_ref[...] = g[tm - 8:tm]
    st_ref[...] = g[tm - (CONV_W - 1):tm]


def ffn_in(a, a_s, w_in, conv_w, conv_b, st0, st1, layer, *, seq, tm, tw):
    M, K = a.shape
    R = a_s.shape[0]
    d_ff = w_in.shape[2] // 2
    nb = M // seq
    tps = seq // tm
    ub0 = d_ff // tw
    kern = functools.partial(_ffn_in_kernel, tiles_per_seq=tps)
    return pl.pallas_call(
        kern,
        grid=(d_ff // tw, M // tm),
        in_specs=[pl.BlockSpec((tm, K), lambda j, i: (i, 0)),
                  pl.BlockSpec((None, K, tw), lambda j, i: (layer, 0, j)),
                  pl.BlockSpec((None, K, tw), lambda j, i: (layer, 0, ub0 + j)),
                  pl.BlockSpec((None, CONV_W, tw), lambda j, i: (layer, 0, j)),
                  pl.BlockSpec((None, 1, tw), lambda j, i: (layer, 0, j)),
                  pl.BlockSpec((R, K), lambda j, i: (0, 0)),
                  pl.BlockSpec((R, tw), lambda j, i: (0, j)),
                  pl.BlockSpec((R, tw), lambda j, i: (0, j))],
        out_specs=[pl.BlockSpec((tm, tw), lambda j, i: (i, j)),
                   pl.BlockSpec((None, CONV_W - 1, tw), lambda j, i: (i // tps, 0, j)),
                   pl.BlockSpec((R, tw), lambda j, i: (0, j)),
                   pl.BlockSpec((R, tw), lambda j, i: (0, j))],
        out_shape=[jax.ShapeDtypeStruct((M, d_ff), BF16),
                   jax.ShapeDtypeStruct((nb, CONV_W - 1, d_ff), F32),
                   jax.ShapeDtypeStruct((R, d_ff), BF16),
                   jax.ShapeDtypeStruct((R, d_ff), F32)],
        scratch_shapes=[pltpu.VMEM((K, tw), BF16), pltpu.VMEM((K, tw), BF16),
                        pltpu.VMEM((8, tw), F32)],
        compiler_params=_params("parallel", "arbitrary"),
        name="ffn_in",
    )(a, w_in, w_in, conv_w, conv_b, a_s, st0, st1)


def _gla_tables(C):
    n_levels = int(math.log2(C))
    assert 1 << n_levels == C
    i = np.arange(C)[:, None]
    t = np.arange(C)[None, :]
    mats = [t <= i]
    level = np.full((C, C), -1, np.int32)
    level[np.arange(C), np.arange(C)] = n_levels
    for lv in range(n_levels):
        s = 1 << lv
        mid = (i // (2 * s)) * (2 * s) + s
        upper = i >= mid
        if 2 * s <= SUBLANES:
            mats.append((upper & (t >= mid) & (t <= i)) | (~upper & (t > i) & (t < mid)))
        same = (i // (2 * s)) == (t // (2 * s))
        own = same & upper & ~((t % (2 * s)) >= s)
        level[own] = lv
    return np.concatenate(mats, axis=0).astype(np.float32), level, n_levels


def _nt(a, b):
    return lax.dot_general(a, b, (((1,), (1,)), ((), ())), preferred_element_type=F32)


def _tn(a, b):
    return lax.dot_general(a, b, (((0,), (0,)), ((), ())), preferred_element_type=F32)


def _gla_kernel(q_ref, k_ref, v_ref, r_ref, gl_ref, w2_ref, bg_ref, ng_ref, s0_ref, ms_ref, lvl_ref,
                o_ref, sout_ref, st_ref, *, C, n_levels, seq_len, dk, dv, heads):
    c = pl.program_id(2)
    nc = pl.num_programs(2)

    @pl.when(c == 0)
    def _():
        for hh in range(heads):
            st_ref[hh] = s0_ref[hh].T

    glb = gl_ref[...].astype(BF16)
    ms = ms_ref[...]
    lvl = lvl_ref[...]
    for hh in range(heads):
        ks = slice(hh * dk, (hh + 1) * dk)
        vs = slice(hh * dv, (hh + 1) * dv)
        z = jnp.dot(glb, w2_ref[:, ks].astype(BF16), preferred_element_type=F32) + bg_ref[:, ks]
        g = -(jnp.maximum(-z, 0.0) + jnp.log1p(jnp.exp(-jnp.abs(z)))) / GLA_GATE_TAU
        if seq_len % C:
            row = c * C + lax.broadcasted_iota(jnp.int32, g.shape, 0)
            g = jnp.where(row < seq_len, g, 0.0)
        g_hi = g.astype(BF16)
        g_lo = (g - g_hi.astype(F32)).astype(BF16)
        ex = jnp.dot(ms, g_hi, preferred_element_type=F32) + jnp.dot(ms, g_lo, preferred_element_type=F32)
        b = ex[0:C]
        e_b = jnp.exp(b)
        e_end = jnp.exp(b[C - 1:C] - b)

        q = q_ref[:, ks] * (dk ** -0.5)
        k = k_ref[:, ks]
        vb = v_ref[:, vs].astype(BF16)
        st = st_ref[hh]
        o = _nt((q * e_b).astype(BF16), st.astype(BF16))
        a = jnp.where(lvl == n_levels, _nt(q.astype(BF16), k.astype(BF16)), 0.0)
        for lv in range(n_levels):
            s = 1 << lv
            if 2 * s <= SUBLANES:
                e_l = jnp.exp(ex[(1 + lv) * C:(2 + lv) * C])
            else:
                m = jnp.broadcast_to(b.reshape(C // (2 * s), 2 * s, dk)[:, s - 1:s, :], (C // (2 * s), 2 * s, dk))
                e_l = jnp.exp(-jnp.abs(b - m.reshape(C, dk)))
            a = jnp.where(lvl == lv, _nt((q * e_l).astype(BF16), (k * e_l).astype(BF16)), a)
        o = o + jnp.dot(a.astype(BF16), vb, preferred_element_type=F32)
        decay_end = e_b[C - 1:C]
        st_ref[hh] = st * decay_end + _tn(vb, (k * e_end).astype(BF16))

        y = o * lax.rsqrt(jnp.mean(o * o, axis=-1, keepdims=True) + EPS) * ng_ref[...]
        o_ref[:, vs] = (y * _silu(r_ref[:, vs])).astype(o_ref.dtype)

    @pl.when(c == nc - 1)
    def _():
        for hh in range(heads):
            sout_ref[hh] = st_ref[hh].T


def gla_core(h, gl, w_gate2, b_gate, norm_gain, s0, layer, s0_layer, *, seq_len, C, n_heads, dk, dv,
             heads_per_step):
    B, Lp, _ = h.shape
    ms_np, lvl_np, n_levels = _gla_tables(C)
    ms = jnp.asarray(ms_np, BF16)
    lvl = jnp.asarray(lvl_np)
    hps = heads_per_step
    assert n_heads % hps == 0
    ng = n_heads // hps
    v0 = 2 * n_heads * dk // (hps * dv)
    r0 = v0 + ng
    kern = functools.partial(_gla_kernel, C=C, n_levels=n_levels, seq_len=seq_len, dk=dk, dv=dv, heads=hps)
    R = ms_np.shape[0]
    return pl.pallas_call(
        kern,
        grid=(B, ng, Lp // C),
        in_specs=[pl.BlockSpec((None, C, hps * dk), lambda b, hd, c: (b, c, hd)),
                  pl.BlockSpec((None, C, hps * dk), lambda b, hd, c: (b, c, ng + hd)),
                  pl.BlockSpec((None, C, hps * dv), lambda b, hd, c: (b, c, v0 + hd)),
                  pl.BlockSpec((None, C, hps * dv), lambda b, hd, c: (b, c, r0 + hd)),
                  pl.BlockSpec((None, C, LANES), lambda b, hd, c: (b, c, 0)),
                  pl.BlockSpec((None, LANES, hps * dk), lambda b, hd, c: (layer, 0, hd)),
                  pl.BlockSpec((None, 1, hps * dk), lambda b, hd, c: (layer, 0, hd)),
                  pl.BlockSpec((None, 1, dv), lambda b, hd, c: (layer, 0, 0)),
                  pl.BlockSpec((None, None, hps, dk, dv), lambda b, hd, c: (s0_layer, b, hd, 0, 0)),
                  pl.BlockSpec((R, C), lambda b, hd, c: (0, 0)),
                  pl.BlockSpec((C, C), lambda b, hd, c: (0, 0))],
        out_specs=[pl.BlockSpec((None, C, hps * dv), lambda b, hd, c: (b, c, hd)),
                   pl.BlockSpec((None, hps, dk, dv), lambda b, hd, c: (b, hd, 0, 0))],
        out_shape=[jax.ShapeDtypeStruct((B, Lp, n_heads * dv), BF16),
                   jax.ShapeDtypeStruct((B, n_heads, dk, dv), F32)],
        scratch_shapes=[pltpu.VMEM((hps, dv, dk), F32)],
        compiler_params=_params("parallel", "parallel", "arbitrary"),
        name="gla_core",
    )(h, h, h, h, gl, w_gate2, b_gate, norm_gain, s0, ms, lvl)


def _rope_tables(pos):
    half = ROT_DIM // 2
    inv = jnp.power(ROPE_THETA, -jnp.arange(half, dtype=F32) * (2.0 / ROT_DIM))
    ang = pos.astype(F32)[:, None] * inv[None, :]
    cos, sin = jnp.cos(ang), jnp.sin(ang)
    n = pos.shape[0]
    cos_t = jnp.concatenate([cos, cos, jnp.ones((n, HEAD_DIM - ROT_DIM), F32)], axis=1)
    sin_t = jnp.concatenate([-sin, sin, jnp.zeros((n, HEAD_DIM - ROT_DIM), F32)], axis=1)
    return cos_t, sin_t


def _qkv_epilogue(product, o_ref, gain_ref, cos_ref, sin_ref, *, tn, section, head_major, row_chunk):
    j = pl.program_id(0)
    kind = (j * tn // section) % 3
    tm = cos_ref.shape[0]
    chunks = [slice(r, r + row_chunk) for r in range(0, tm, row_chunk)]

    def put(rows, hh, val):
        if head_major:
            o_ref[hh, rows, :] = val
        else:
            o_ref[rows, hh * HEAD_DIM:(hh + 1) * HEAD_DIM] = val

    @pl.when(kind == 2)
    def _():
        for rows in chunks:
            acc = product(rows)
            for hh in range(tn // HEAD_DIM):
                put(rows, hh, acc[:, hh * HEAD_DIM:(hh + 1) * HEAD_DIM])

    @pl.when(kind != 2)
    def _():
        half = ROT_DIM // 2
        gain = gain_ref[...]
        lane = lax.broadcasted_iota(jnp.int32, (row_chunk, HEAD_DIM), 1)
        for rows in chunks:
            acc = product(rows)
            cos_t = cos_ref[rows, :]
            sin_t = sin_ref[rows, :]
            for hh in range(tn // HEAD_DIM):
                x = acc[:, hh * HEAD_DIM:(hh + 1) * HEAD_DIM]
                y = x * lax.rsqrt(jnp.mean(x * x, axis=-1, keepdims=True) + EPS) * gain
                swapped = jnp.where(lane < half, pltpu.roll(y, HEAD_DIM - half, axis=1),
                                    pltpu.roll(y, half, axis=1))
                put(rows, hh, y * cos_t + swapped * sin_t)


def dsa_qkv(a, a_s, w_qkv, gains, rope, rope_s, layer, *, tm, tn, section, seq):
    M = a.shape[0]
    R = a_s.shape[0]
    tps = seq // tm
    n_sec = w_qkv.shape[2] // section
    gain_spec = pl.BlockSpec((None, 1, HEAD_DIM), lambda j, i: (layer * n_sec + j * tn // section, 0, 0))
    extra_specs = [gain_spec,
                   pl.BlockSpec((tm, HEAD_DIM), lambda j, i: (i % tps, 0)),
                   pl.BlockSpec((tm, HEAD_DIM), lambda j, i: (i % tps, 0))]
    extra_specs_s = [gain_spec,
                     pl.BlockSpec((R, HEAD_DIM), lambda j, i: (0, 0)),
                     pl.BlockSpec((R, HEAD_DIM), lambda j, i: (0, 0))]
    epi = functools.partial(_qkv_epilogue, tn=tn, section=section, head_major=True, row_chunk=min(256, tm))
    epi_s = functools.partial(_qkv_epilogue, tn=tn, section=section, head_major=False, row_chunk=R)
    hpb = tn // HEAD_DIM
    out_spec = pl.BlockSpec((None, hpb, tm, HEAD_DIM), lambda j, i: (i // tps, j, i % tps, 0))
    out_shape = jax.ShapeDtypeStruct((M // seq, w_qkv.shape[2] // HEAD_DIM, seq, HEAD_DIM), F32)
    return mm_wres(a, a_s, w_qkv, layer, tm=tm, tn=tn, epilogue=epi, epilogue_s=epi_s,
                   extra=(gains, *rope), extra_specs=extra_specs,
                   extra_s=(gains, *rope_s), extra_specs_s=extra_specs_s,
                   out_spec=out_spec, out_shape=out_shape, name="dsa_qkv")


def _rows(start, size, stride):
    if stride == 1:
        return (pl.ds(start, size), slice(None))
    return (pl.ds(start, size, stride=stride), slice(None))


def _prompt_attn_kernel(*refs, seq, merge_rows):
    qkv_refs = refs[:3 * N_GROUPS]
    out_ref = refs[3 * N_GROUPS]
    o_scr = refs[3 * N_GROUPS + 1:3 * N_GROUPS + 1 + N_GROUPS]
    l_scr = refs[3 * N_GROUPS + 1 + N_GROUPS:]
    scale = HEAD_DIM ** -0.5
    for g, (window, dil) in enumerate(DSA_GROUPS):
        q_ref, k_ref, v_ref = qkv_refs[3 * g:3 * g + 3]
        blk = window // dil
        span = blk * dil
        nb = seq // span
        qi = lax.broadcasted_iota(jnp.int32, (blk, 2 * blk), 0)
        ki = lax.broadcasted_iota(jnp.int32, (blk, 2 * blk), 1)
        both_ok = (ki >= qi) & (ki <= qi + blk)
        first_ok = (lax.broadcasted_iota(jnp.int32, (blk, blk), 1)
                    <= lax.broadcasted_iota(jnp.int32, (blk, blk), 0))
        for r in range(dil):
            for n in range(nb):
                start = r + n * span
                q = q_ref[_rows(start, blk, dil)].astype(BF16)
                if n == 0:
                    kk = k_ref[_rows(start, blk, dil)].astype(BF16)
                    vv = v_ref[_rows(start, blk, dil)].astype(BF16)
                    ok = first_ok
                else:
                    kk = k_ref[_rows(start - span, 2 * blk, dil)].astype(BF16)
                    vv = v_ref[_rows(start - span, 2 * blk, dil)].astype(BF16)
                    ok = both_ok
                s = jnp.where(ok, _nt(q, kk) * scale, -jnp.inf)
                mx = jnp.max(s, axis=-1, keepdims=True)
                p = jnp.exp(s - mx)
                den = jnp.sum(p, axis=-1, keepdims=True)
                o = jnp.dot(p.astype(BF16), vv, preferred_element_type=F32) / den
                o_scr[g][_rows(start, blk, dil)] = o
                l_scr[g][_rows(start, blk, dil)] = jnp.broadcast_to(mx + jnp.log(den), (blk, HEAD_DIM))
    for c in range(seq // merge_rows):
        sl = slice(c * merge_rows, (c + 1) * merge_rows)
        ls = [l_scr[g][sl, :] for g in range(N_GROUPS)]
        mx = functools.reduce(jnp.maximum, ls)
        es = [jnp.exp(l - mx) for l in ls]
        den = functools.reduce(lambda a, b: a + b, es)
        acc = es[0] * o_scr[0][sl, :]
        for g in range(1, N_GROUPS):
            acc = acc + es[g] * o_scr[g][sl, :]
        out_ref[sl, :] = (acc / den).astype(out_ref.dtype)


def prompt_attention(qkvh, *, n_heads):
    B, _, S, _ = qkvh.shape
    for window, dil in DSA_GROUPS:
        assert S % window == 0
    in_specs = []
    for g in range(N_GROUPS):
        for t in range(3):
            in_specs.append(pl.BlockSpec((None, None, S, HEAD_DIM),
                                         lambda b, h, g=g, t=t: (b, (g * 3 + t) * n_heads + h, 0, 0)))
    kern = functools.partial(_prompt_attn_kernel, seq=S, merge_rows=min(256, S))
    return pl.pallas_call(
        kern,
        grid=(B, n_heads),
        in_specs=in_specs,
        out_specs=pl.BlockSpec((None, S, HEAD_DIM), lambda b, h: (b, 0, h)),
        out_shape=jax.ShapeDtypeStruct((B, S, n_heads * HEAD_DIM), BF16),
        scratch_shapes=[pltpu.VMEM((S, HEAD_DIM), F32)] * (2 * N_GROUPS),
        compiler_params=_params("parallel", "parallel"),
        name="prompt_attention",
    )(*([qkvh] * (3 * N_GROUPS)))


def _kv_export_kernel(*refs, n_layers, n_heads, ts):
    out_ref = refs[-1]
    layer = pl.program_id(0)
    for l in range(n_layers):
        k_ref, v_ref = refs[2 * l], refs[2 * l + 1]

        @pl.when(layer == l)
        def _():
            kt = jnp.swapaxes(k_ref[...], 0, 1)
            vt = jnp.swapaxes(v_ref[...], 0, 1)
            out_ref[...] = jnp.concatenate([kt, vt], axis=1).reshape(ts * 2 * n_heads, HEAD_DIM)


def kv_export_sc(qkvh, group, *, keep, n_heads, n_layers):
    B, _, S, _ = qkvh.shape
    rows = 2 * n_heads
    h0 = (group * 3 + 1) * n_heads
    mesh = plsc.VectorSubcoreMesh(core_axis_name="core", subcore_axis_name="subcore")
    n_workers = mesh.num_cores * mesh.num_subcores
    n_items = B * keep

    @pl.kernel(out_type=jax.ShapeDtypeStruct((n_layers, B, keep, rows, HEAD_DIM), F32), mesh=mesh,
               scratch_types=[pltpu.VMEM((rows, HEAD_DIM), F32)])
    def export(x_hbm, o_hbm, buf):
        worker = lax.axis_index("core") * mesh.num_subcores + lax.axis_index("subcore")

        @pl.loop(0, pl.cdiv(n_items, n_workers))
        def _(it):
            q = it * n_workers + worker

            @pl.when(q < n_items)
            def _():
                b = q // keep
                t = q % keep
                pltpu.sync_copy(x_hbm.at[b, pl.ds(h0, rows), S - keep + t], buf)
                pltpu.sync_copy(buf, o_hbm.at[0, b, t])

    return export(qkvh)


def kv_export(qkvh_layers, group, *, keep, n_heads, ts, into=None, first_layer=0):
    n_layers = len(qkvh_layers)
    B, _, S, _ = qkvh_layers[0].shape
    nt = keep // ts
    t0 = (S - keep) // ts
    in_specs, args = [], []
    for l, arr in enumerate(qkvh_layers):
        for t in (1, 2):
            def idx(ll, b, i, l=l, t=t):
                bb = jnp.where(ll == l, b, jnp.where(ll < l, 0, B - 1))
                ii = jnp.where(ll == l, i, jnp.where(ll < l, 0, nt - 1))
                return (bb, group * 3 + t, t0 + ii, 0)
            in_specs.append(pl.BlockSpec((None, n_heads, ts, HEAD_DIM), idx))
            args.append(arr)
    rows = 2 * n_heads
    total_layers = first_layer + n_layers
    aliases = {}
    if into is not None:
        in_specs.append(pl.BlockSpec(memory_space=pl.ANY))
        args.append(into.reshape(total_layers, B, keep * rows, HEAD_DIM))
        aliases = {len(args) - 1: 0}
    out = pl.pallas_call(
        functools.partial(_kv_export_kernel, n_layers=n_layers, n_heads=n_heads, ts=ts),
        grid=(n_layers, B, nt),
        in_specs=in_specs,
        out_specs=pl.BlockSpec((None, None, ts * rows, HEAD_DIM), lambda ll, b, i: (first_layer + ll, b, i, 0)),
        out_shape=jax.ShapeDtypeStruct((total_layers, B, keep * rows, HEAD_DIM), F32),
        input_output_aliases=aliases,
        compiler_params=_params("arbitrary", "arbitrary", "arbitrary"),
        name=f"kv_export_g{group}",
    )(*args)
    return out.reshape(total_layers, B, keep, 2, n_heads, HEAD_DIM)


def _sample_attn_kernel(qkv_ref, c0_ref, c1_ref, c2_ref, out_ref, *, n_heads):
    scale = HEAD_DIM ** -0.5
    os, ls = [], []
    for g, c_ref in enumerate((c0_ref, c1_ref, c2_ref)):
        q, k_new, v_new = qkv_ref[3 * g], qkv_ref[3 * g + 1], qkv_ref[3 * g + 2]
        kc = c_ref[:, 0:n_heads, :]
        vc = c_ref[:, n_heads:2 * n_heads, :]
        s_c = jnp.sum(kc * q[None], axis=-1, keepdims=True) * scale
        s_n = jnp.sum(k_new * q, axis=-1, keepdims=True) * scale
        mx = jnp.maximum(jnp.max(s_c, axis=0), s_n)
        p_c = jnp.exp(s_c - mx[None])
        p_n = jnp.exp(s_n - mx)
        den = jnp.sum(p_c, axis=0) + p_n
        os.append((jnp.sum(p_c * vc, axis=0) + p_n * v_new) / den)
        ls.append(mx + jnp.log(den))
    mx = functools.reduce(jnp.maximum, ls)
    es = [jnp.exp(l - mx) for l in ls]
    den = functools.reduce(lambda a, b: a + b, es)
    acc = es[0] * os[0]
    for g in range(1, N_GROUPS):
        acc = acc + es[g] * os[g]
    out_ref[...] = (acc / den).astype(out_ref.dtype)


def sample_attention(qkv, caches, layer, *, n_heads):
    B = qkv.shape[0]
    rows = 2 * n_heads
    views, specs = [], []
    for (window, dil), c in zip(DSA_GROUPS, caches):
        n_buf = c.shape[2]
        assert n_buf == window and n_buf % dil == 0
        views.append(c.reshape(c.shape[0], B, n_buf // dil, dil * rows, HEAD_DIM))
        specs.append(pl.BlockSpec((None, None, n_buf // dil, rows, HEAD_DIM), lambda b: (layer, b, 0, 0, 0)))
    return pl.pallas_call(
        functools.partial(_sample_attn_kernel, n_heads=n_heads),
        grid=(B,),
        in_specs=[pl.BlockSpec((None, 3 * N_GROUPS, n_heads, HEAD_DIM), lambda b: (b, 0, 0, 0))] + specs,
        out_specs=pl.BlockSpec((None, n_heads, HEAD_DIM), lambda b: (b, 0, 0)),
        out_shape=jax.ShapeDtypeStruct((B, n_heads, HEAD_DIM), F32),
        compiler_params=_params("parallel"),
        name="sample_attention",
    )(qkv, *views)


def _set_last_kernel(rolled_ref, new_ref, out_ref):
    del rolled_ref
    out_ref[...] = new_ref[...]


def cache_shift(cache, new):
    NL, B, n_buf = cache.shape[:3]
    rows = cache.shape[3] * cache.shape[4]
    LB = NL * B
    c4 = cache.reshape(LB, n_buf, rows, HEAD_DIM)
    n4 = new.reshape(LB, 1, rows, HEAD_DIM)
    mesh = plsc.VectorSubcoreMesh(core_axis_name="core", subcore_axis_name="subcore")
    n_workers = mesh.num_cores * mesh.num_subcores
    n_moves = LB * (n_buf - 1)

    @pl.kernel(out_type=jax.ShapeDtypeStruct(c4.shape, c4.dtype), mesh=mesh,
               scratch_types=[pltpu.VMEM((rows, HEAD_DIM), c4.dtype)])
    def roll(x_hbm, o_hbm, buf):
        worker = lax.axis_index("core") * mesh.num_subcores + lax.axis_index("subcore")

        @pl.loop(0, pl.cdiv(n_moves, n_workers))
        def _(it):
            q = it * n_workers + worker

            @pl.when(q < n_moves)
            def _():
                lb = q // (n_buf - 1)
                p = q % (n_buf - 1)
                pltpu.sync_copy(x_hbm.at[lb, p + 1], buf)
                pltpu.sync_copy(buf, o_hbm.at[lb, p])

    rolled = roll(c4)
    out = pl.pallas_call(
        _set_last_kernel,
        grid=(LB,),
        in_specs=[pl.BlockSpec((None, 1, rows, HEAD_DIM), lambda lb: (lb, n_buf - 1, 0, 0)),
                  pl.BlockSpec((None, 1, rows, HEAD_DIM), lambda lb: (lb, 0, 0, 0))],
        out_specs=pl.BlockSpec((None, 1, rows, HEAD_DIM), lambda lb: (lb, n_buf - 1, 0, 0)),
        out_shape=jax.ShapeDtypeStruct(c4.shape, c4.dtype),
        input_output_aliases={0: 0},
        compiler_params=_params("parallel"),
        name="cache_set_last",
    )(rolled, n4)
    return out.reshape(cache.shape)


def kernel(x_prompt, x_sample, cache_kv_w128, cache_kv_w512, cache_kv_w2048, state_gla, state_ffn_conv,
           norm_mix, norm_ffn, dsa_w_qkv, dsa_q_gain, dsa_k_gain, dsa_w_o,
           gla_w_in, gla_w_gate2, gla_b_gate, gla_norm_gain, gla_w_o,
           ffn_w_in, ffn_conv_w, ffn_conv_b, ffn_w_out):
    B, S, D = x_prompt.shape
    BS = x_sample.shape[0]
    assert x_sample.shape[1] == 1
    depth = norm_mix.shape[0]
    caches = (cache_kv_w128, cache_kv_w512, cache_kv_w2048)
    n_gla, gla_heads, gla_dk, gla_dv = state_gla.shape[0], state_gla.shape[2], state_gla.shape[3], state_gla.shape[4]
    gla_kd, gla_vd = gla_heads * gla_dk, gla_heads * gla_dv
    dsa_heads = dsa_w_o.shape[1] // HEAD_DIM
    W = dsa_heads * HEAD_DIM
    d_ff = ffn_w_in.shape[2] // 2
    M = B * S
    R = SAMPLE_ROWS

    tm = min(1024, S)
    tn = min(512, W)
    tw = 256
    tm_rms = min(512, M)
    gla_chunk = min(256, S)
    gla_hps = min(4, gla_heads)

    xp = x_prompt.reshape(M, D)
    xs = jnp.pad(x_sample.reshape(BS, D), ((0, R - BS), (0, 0)))

    norm_mix3 = norm_mix.reshape(depth, 1, D)
    norm_ffn3 = norm_ffn.reshape(depth, 1, D)
    conv_b3 = ffn_conv_b.reshape(depth, 1, d_ff)
    n_main = 2 * gla_kd + 2 * gla_vd
    w_gl = jnp.pad(gla_w_in[:, :, n_main:], ((0, 0), (0, 0), (0, LANES - GLA_GATE_RANK)))
    gla_w_in_t = jnp.swapaxes(gla_w_in, 1, 2)
    w_gate2 = jnp.pad(gla_w_gate2, ((0, 0), (0, LANES - GLA_GATE_RANK), (0, 0)))
    b_gate3 = gla_b_gate.reshape(n_gla, 1, gla_kd)
    gla_ng3 = gla_norm_gain.reshape(n_gla, 1, gla_dv)
    ones = jnp.ones_like(dsa_q_gain)
    qk_gains = jnp.stack([dsa_q_gain, dsa_k_gain, ones], axis=2).reshape(-1, 1, HEAD_DIM)
    rope_p = _rope_tables(jnp.arange(S))
    rope_s = _rope_tables(jnp.full((R,), PAST_LEN, jnp.int32))

    qkvh_layers = []
    kv_new = [[] for _ in DSA_GROUPS]
    gla_p, gla_s, conv_p, conv_s = [], [], [], []
    zero_state = jnp.zeros((1, B, gla_heads, gla_dk, gla_dv), F32)

    for i in range(depth):
        j = i // 2
        hp = rms_bf16(xp, norm_mix3, i, tm_rms)
        hs = rms_bf16(xs, norm_mix3, i, R)
        if i % 2 == 0:
            h, h_s = mm_wres(hp, hs, gla_w_in_t, j, tm=tm, tn=tn, ncols=n_main, w_transposed=True, name="gla_in")
            gl, gl_s = mm_wres(hp, hs, w_gl, j, tm=tm, tn=LANES, name="gla_gate_in")
            o, sp = gla_core(h.reshape(B, S, n_main), gl.reshape(B, S, LANES), w_gate2, b_gate3, gla_ng3,
                             zero_state, j, 0, seq_len=S, C=gla_chunk, n_heads=gla_heads, dk=gla_dk, dv=gla_dv,
                             heads_per_step=gla_hps)
            h_s = jnp.pad(h_s[:BS].reshape(BS, 1, n_main), ((0, 0), (0, R - 1), (0, 0)))
            gl_s = jnp.pad(gl_s[:BS].reshape(BS, 1, LANES), ((0, 0), (0, R - 1), (0, 0)))
            o_s, ss = gla_core(h_s, gl_s, w_gate2, b_gate3, gla_ng3, state_gla, j, j,
                               seq_len=1, C=R, n_heads=gla_heads, dk=gla_dk, dv=gla_dv, heads_per_step=gla_hps)
            o_s = jnp.pad(o_s[:, 0], ((0, R - BS), (0, 0)))
            xp, xs = mm_wres(o.reshape(M, gla_vd), o_s, gla_w_o, j, tm=tm, tn=tn, residual=xp, residual_s=xs,
                             name="gla_out")
            gla_p.append(sp)
            gla_s.append(ss)
        else:
            qkvh, qkv_s = dsa_qkv(hp, hs, dsa_w_qkv, qk_gains, rope_p, rope_s, j, tm=tm, tn=tn, section=W, seq=S)
            merged = prompt_attention(qkvh, n_heads=dsa_heads)
            qkvh_layers.append(qkvh)
            if len(qkvh_layers) == 1:
                kv_first = [kv_export_sc(qkvh, g, keep=min(window, S), n_heads=dsa_heads, n_layers=depth // 2)
                            for g, (window, _) in enumerate(DSA_GROUPS)]
            qkv_s = qkv_s[:BS].reshape(BS, N_GROUPS, 3, dsa_heads, HEAD_DIM)
            merged_s = sample_attention(qkv_s.reshape(BS, 3 * N_GROUPS, dsa_heads, HEAD_DIM), caches, j,
                                        n_heads=dsa_heads)
            merged_s = jnp.pad(merged_s.reshape(BS, W).astype(BF16), ((0, R - BS), (0, 0)))
            xp, xs = mm_wres(merged.reshape(M, W), merged_s, dsa_w_o, j, tm=tm, tn=tn, residual=xp,
                             residual_s=xs, name="dsa_out")
            for g in range(N_GROUPS):
                kv_new[g].append(qkv_s[:, g, 1:3])
        fp = rms_bf16(xp, norm_ffn3, i, tm_rms)
        fs = rms_bf16(xs, norm_ffn3, i, R)
        st = jnp.pad(state_ffn_conv[i], ((0, R - BS), (0, 0), (0, 0)))
        act, cp, act_s, g_s = ffn_in(fp, fs, ffn_w_in, ffn_conv_w, conv_b3, st[:, 0], st[:, 1], i,
                                     seq=S, tm=tm, tw=tw)
        xp, xs = mm_fullk(act, act_s, ffn_w_out, i, xp, xs, tm=tm, tn=tw, name="ffn_out")
        conv_p.append(cp)
        conv_s.append(jnp.stack([state_ffn_conv[i][:, 1], g_s[:BS]], axis=1))

    kv128_p, kv512_p, kv2048_p = [
        kv_export(qkvh_layers[1:], g, keep=min(window, S), n_heads=dsa_heads, ts=min(256, window, S),
                  into=kv_first[g], first_layer=1)
        for g, (window, _) in enumerate(DSA_GROUPS)]
    kv128_s, kv512_s, kv2048_s = [cache_shift(c, jnp.stack(n, axis=0))
                                  for c, n in zip(caches, kv_new)]
    return (xp.reshape(B, S, D), xs[:BS].reshape(BS, 1, D),
            kv128_p, kv128_s, kv512_p, kv512_s, kv2048_p, kv2048_s,
            jnp.stack(gla_p, axis=0), jnp.stack(gla_s, axis=0),
            jnp.stack(conv_p, axis=0), jnp.stack(conv_s, axis=0))
```
